```python
import math
import jax, jax.numpy as jnp
from jax import lax
import numpy as np

D_MODEL = 1024
BATCH = 32
SEQ = 2048
DEPTH = 1

SSM_WIDTH = D_MODEL // 2
SSM_GROUP = 16
SSM_GROUPS = SSM_WIDTH // SSM_GROUP
SSM_STATE = 64
ATTN_HEADS = 8
HEAD_DIM = (D_MODEL // 2) // ATTN_HEADS
ATTN_WIDTH = ATTN_HEADS * HEAD_DIM
MOBA_BLOCK = 256
MOBA_TOPK = 3
Q_CHUNK = 128
D_FF = 4 * D_MODEL
PLE_DIM = 256
RMS_EPS = 1e-6
DT_MIN = 1e-3
DT_MAX = 1e-1
NEG_INF = -1e30
OFF_U = 0
OFF_Q = OFF_U + SSM_WIDTH
OFF_K = OFF_Q + ATTN_WIDTH
OFF_V = OFF_K + ATTN_WIDTH
OFF_GA = OFF_V + ATTN_WIDTH
OFF_GB = OFF_GA + D_MODEL
IN_COLS = OFF_GB + D_MODEL

kernel_name = "hybrid_s5_moba_sqrelu_block"


def rmsnorm(x, g):
    xf = x.astype(jnp.float32)
    y = xf * lax.rsqrt(jnp.mean(xf * xf, axis=-1, keepdims=True) + RMS_EPS)
    return (y * g.astype(jnp.float32)).astype(x.dtype)


def _ssm_combine(e1, e2):
    a1, b1 = e1
    a2, b2 = e2
    return a1 * a2, a2 * b1 + b2


def s5_mixer(u, lam_re, lam_im, log_dt, b_re, b_im, c_re, c_im, d_skip, w_glu, b_glu):
    f32 = jnp.float32
    bsz, seq, _ = u.shape
    uf = u.astype(f32).reshape(bsz, seq, SSM_GROUPS, SSM_GROUP)
    lam = lax.complex(lam_re.astype(f32), lam_im.astype(f32))
    dt = jnp.exp(log_dt.astype(f32))[:, None]
    lam_bar = jnp.exp(lam * dt)
    b_mat = lax.complex(b_re.astype(f32), b_im.astype(f32))
    b_bar = ((lam_bar - 1.0) / lam)[..., None] * b_mat
    bu = jnp.einsum('bsgc,gpc->bsgp', uf.astype(jnp.complex64), b_bar)
    a = jnp.broadcast_to(lam_bar, (1, seq, SSM_GROUPS, SSM_STATE))
    _, states = lax.associative_scan(_ssm_combine, (a, bu), axis=1)
    c_mat = lax.complex(c_re.astype(f32), c_im.astype(f32))
    y = jnp.real(jnp.einsum('gcp,bsgp->bsgc', c_mat, states))
    y = y + d_skip.astype(f32).reshape(SSM_GROUPS, SSM_GROUP) * uf
    y = jax.nn.gelu(y.reshape(bsz, seq, SSM_WIDTH).astype(u.dtype))
    return y * jax.nn.sigmoid(y @ w_glu + b_glu)


def moba_attention(q, k, v):
    f32 = jnp.float32
    bsz, seq, nh, hd = q.shape
    nb = -(-seq // MOBA_BLOCK)
    s_pad = nb * MOBA_BLOCK
    pad = ((0, 0), (0, s_pad - seq), (0, 0), (0, 0))
    q = jnp.pad(q, pad)
    k = jnp.pad(k, pad)
    v = jnp.pad(v, pad)
    kb = k.reshape(bsz, nb, MOBA_BLOCK, nh, hd).transpose(0, 3, 1, 2, 4)
    vb = v.reshape(bsz, nb, MOBA_BLOCK, nh, hd).transpose(0, 3, 1, 2, 4)
    scale = hd ** -0.5
    n_sel = min(MOBA_TOPK, nb - 1)
    n_chunks = s_pad // Q_CHUNK
    qc = q.reshape(bsz * n_chunks, Q_CHUNK, nh, hd)
    b_ids = jnp.repeat(jnp.arange(bsz, dtype=jnp.int32), n_chunks)
    c_ids = jnp.tile(jnp.arange(n_chunks, dtype=jnp.int32), bsz)
    if n_sel > 0:
        k_mean = jnp.mean(kb.astype(f32), axis=3)
        gate = jnp.einsum('bshd,bhnd->bshn', q.astype(f32), k_mean)
        qblk = jnp.arange(s_pad) // MOBA_BLOCK
        past = jnp.arange(nb)[None, :] < qblk[:, None]
        gate = jnp.where(past[None, :, None, :], gate, NEG_INF)
        _, sel = lax.top_k(gate, n_sel)
        sel_c = sel.astype(jnp.int32).reshape(bsz * n_chunks, Q_CHUNK, nh, n_sel)
    else:
        sel_c = jnp.zeros((bsz * n_chunks, Q_CHUNK, nh, 0), jnp.int32)
    heads = jnp.arange(nh)[None, :, None]

    def step(args):
        qq, ss, b, c = args
        k_b = kb[b]
        v_b = vb[b]
        own = (c * Q_CHUNK) // MOBA_BLOCK
        pos_q = c * Q_CHUNK + jnp.arange(Q_CHUNK)
        pos_k = own * MOBA_BLOCK + jnp.arange(MOBA_BLOCK)
        k_own = k_b[:, own].astype(f32)
        v_own = v_b[:, own].astype(f32)
        qf = qq.astype(f32) * scale
        s_own = jnp.einsum('qhd,hkd->qhk', qf, k_own)
        causal = (pos_k[None, :] <= pos_q[:, None])[:, None, :]
        s_own = jnp.where(causal, s_own, NEG_INF)
        if n_sel > 0:
            k_sel = k_b[heads, ss].astype(f32)
            v_sel = v_b[heads, ss].astype(f32)
            s_sel = jnp.einsum('qhd,qhnkd->qhnk', qf, k_sel)
            valid = (ss < own)[..., None]
            s_sel = jnp.where(valid, s_sel, NEG_INF)
            n_k = n_sel * MOBA_BLOCK
            scores = jnp.concatenate([s_sel.reshape(Q_CHUNK, nh, n_k), s_own], axis=-1)
            probs = jax.nn.softmax(scores, axis=-1)
            p_sel = probs[..., :n_k].reshape(Q_CHUNK, nh, n_sel, MOBA_BLOCK)
            p_own = probs[..., n_k:]
            out = (jnp.einsum('qhnk,qhnkd->qhd', p_sel, v_sel)
                   + jnp.einsum('qhk,hkd->qhd', p_own, v_own))
        else:
            probs = jax.nn.softmax(s_own, axis=-1)
            out = jnp.einsum('qhk,hkd->qhd', probs, v_own)
        return out.astype(qq.dtype)

    outs = lax.map(step, (qc, sel_c, b_ids, c_ids))
    outs = outs.reshape(bsz, s_pad, nh * hd)
    return outs[:, :seq]


def setup_inputs(seed: int = 0) -> dict:
    key = jax.random.key(seed)
    ks = jax.random.split(key, 26)
    f32 = jnp.float32
    L = DEPTH

    def nrm(k, shape, scale):
        return jax.random.normal(k, shape, f32) * scale

    def gain(k):
        return 1.0 + 0.01 * jax.random.normal(k, (L, D_MODEL), f32)

    n_idx = jnp.arange(SSM_STATE, dtype=f32)
    return {
        "x": nrm(ks[0], (BATCH, SEQ, D_MODEL), 1.0),
        "p": nrm(ks[1], (L, BATCH, SEQ, PLE_DIM), 1.0),
        "g_pre_mix": gain(ks[2]),
        "w_in": nrm(ks[3], (L, D_MODEL, IN_COLS), D_MODEL ** -0.5),
        "ssm_lam_re": -0.5 + 0.01 * jax.random.normal(ks[4], (L, SSM_GROUPS, SSM_STATE), f32),
        "ssm_lam_im": math.pi * n_idx + 0.01 * jax.random.normal(ks[5], (L, SSM_GROUPS, SSM_STATE), f32),
        "ssm_log_dt": jax.random.uniform(ks[6], (L, SSM_GROUPS), f32, math.log(DT_MIN), math.log(DT_MAX)),
        "ssm_b_re": nrm(ks[7], (L, SSM_GROUPS, SSM_STATE, SSM_GROUP), (2 * SSM_GROUP) ** -0.5),
        "ssm_b_im": nrm(ks[8], (L, SSM_GROUPS, SSM_STATE, SSM_GROUP), (2 * SSM_GROUP) ** -0.5),
        "ssm_c_re": nrm(ks[9], (L, SSM_GROUPS, SSM_GROUP, SSM_STATE), (2 * SSM_STATE) ** -0.5),
        "ssm_c_im": nrm(ks[10], (L, SSM_GROUPS, SSM_GROUP, SSM_STATE), (2 * SSM_STATE) ** -0.5),
        "ssm_d": nrm(ks[11], (L, SSM_WIDTH), 1.0),
        "w_glu": nrm(ks[12], (L, SSM_WIDTH, SSM_WIDTH), SSM_WIDTH ** -0.5),
        "b_glu": nrm(ks[13], (L, SSM_WIDTH), 0.01),
        "w_branch_a": nrm(ks[14], (L, SSM_WIDTH, D_MODEL), SSM_WIDTH ** -0.5),
        "w_branch_b": nrm(ks[15], (L, ATTN_WIDTH, D_MODEL), ATTN_WIDTH ** -0.5),
        "w_out": nrm(ks[16], (L, D_MODEL, D_MODEL), D_MODEL ** -0.5),
        "g_post_mix": gain(ks[17]),
        "g_pre_mlp": gain(ks[18]),
        "w_mlp1": nrm(ks[19], (L, D_MODEL, D_FF), D_MODEL ** -0.5),
        "w_mlp2": nrm(ks[20], (L, D_FF, D_MODEL), D_FF ** -0.5),
        "g_post_mlp": gain(ks[21]),
        "w_ple": nrm(ks[22], (L, PLE_DIM, D_MODEL), PLE_DIM ** -0.5),
        "w_ple_gate": nrm(ks[23], (L, D_MODEL, D_MODEL), D_MODEL ** -0.5),
        "g_ple": gain(ks[24]),
    }


def reference(x, p, g_pre_mix, w_in, ssm_lam_re, ssm_lam_im, ssm_log_dt, ssm_b_re, ssm_b_im,
              ssm_c_re, ssm_c_im, ssm_d, w_glu, b_glu, w_branch_a, w_branch_b, w_out,
              g_post_mix, g_pre_mlp, w_mlp1, w_mlp2, g_post_mlp, w_ple, w_ple_gate, g_ple):
    bsz, seq, _ = x.shape
    for i in range(DEPTH):
        h = rmsnorm(x, g_pre_mix[i])
        z = h @ w_in[i]
        u = z[..., OFF_U:OFF_Q]
        q = z[..., OFF_Q:OFF_K].reshape(bsz, seq, ATTN_HEADS, HEAD_DIM)
        k = z[..., OFF_K:OFF_V].reshape(bsz, seq, ATTN_HEADS, HEAD_DIM)
        v = z[..., OFF_V:OFF_GA].reshape(bsz, seq, ATTN_HEADS, HEAD_DIM)
        gate_a = jax.nn.sigmoid(z[..., OFF_GA:OFF_GB])
        gate_b = jax.nn.sigmoid(z[..., OFF_GB:IN_COLS])
        y_a = s5_mixer(u, ssm_lam_re[i], ssm_lam_im[i], ssm_log_dt[i], ssm_b_re[i], ssm_b_im[i],
                       ssm_c_re[i], ssm_c_im[i], ssm_d[i], w_glu[i], b_glu[i]) @ w_branch_a[i]
        y_b = moba_attention(q, k, v) @ w_branch_b[i]
        mixed = (gate_a * y_a + gate_b * y_b) @ w_out[i]
        x = x + rmsnorm(mixed, g_post_mix[i])
        hm = rmsnorm(x, g_pre_mlp[i])
        f = jnp.square(jax.nn.relu(hm @ w_mlp1[i])) @ w_mlp2[i]
        x = x + rmsnorm(f, g_post_mlp[i])
        e = (p[i] @ w_ple[i]) * jax.nn.sigmoid(x @ w_ple_gate[i])
        x = x + rmsnorm(e, g_ple[i])
    return x
```

```python
import functools

import jax
import jax.numpy as jnp
from jax import lax
from jax.experimental import pallas as pl
from jax.experimental.pallas import tpu as pltpu

F32 = jnp.float32
BF16 = jnp.bfloat16

SSM_GROUP = 16
SSM_STATE = 64
SSM_CHUNK = 16
ATTN_HEADS = 8
MOBA_BLOCK = 256
MOBA_TOPK = 3
RMS_EPS = 1e-6
NEG_INF = -1e30
HEAD_PAIR = 2
TOKEN_TILE = 512
FF_CHUNK = 1024
VMEM_LIMIT = 48 * 1024 * 1024

_NT = (((1,), (1,)), ((), ()))


def _rms(v):
    return v * lax.rsqrt(jnp.mean(v * v, axis=-1, keepdims=True) + RMS_EPS)


def _const_spec(shape):
    nd = len(shape)
    return pl.BlockSpec(shape, lambda *_: (0,) * nd, pipeline_mode=pl.Buffered(1))


def _params(sem):
    return pltpu.CompilerParams(dimension_semantics=sem, vmem_limit_bytes=VMEM_LIMIT)


def _in_proj_kernel(x_ref, g_ref, wu_ref, wqt_ref, wk_ref, wvt_ref, wg_ref,
                    u_ref, qt_ref, k_ref, vt_ref, ga_ref, gb_ref, *, q_scale):
    x = x_ref[0]
    hb = (_rms(x) * g_ref[...]).astype(BF16)
    d = x.shape[-1]
    u_ref[0] = jnp.dot(hb, wu_ref[...], preferred_element_type=F32).astype(BF16)
    k_ref[0] = jnp.dot(hb, wk_ref[...], preferred_element_type=F32).astype(BF16)
    qt = lax.dot_general(wqt_ref[...], hb, _NT, preferred_element_type=F32)
    qt_ref[0] = (qt * q_scale).astype(BF16)
    vt_ref[0] = lax.dot_general(wvt_ref[...], hb, _NT, preferred_element_type=F32).astype(BF16)
    gates = jax.nn.sigmoid(jnp.dot(hb, wg_ref[...], preferred_element_type=F32))
    ga_ref[0] = gates[:, :d].astype(BF16)
    gb_ref[0] = gates[:, d:].astype(BF16)


def _in_proj(x, g, wu, wqt, wk, wvt, wg, q_scale):
    b, s, d = x.shape
    wdt = wu.shape[1]
    a = wk.shape[1]
    tm = min(TOKEN_TILE, s)
    tok = lambda w: pl.BlockSpec((1, tm, w), lambda bi, ti: (bi, ti, 0))
    tr = lambda w: pl.BlockSpec((1, w, tm), lambda bi, ti: (bi, 0, ti))
    return pl.pallas_call(
        functools.partial(_in_proj_kernel, q_scale=q_scale),
        grid=(b, s // tm),
        in_specs=[tok(d), _const_spec(g.shape), _const_spec(wu.shape), _const_spec(wqt.shape),
                  _const_spec(wk.shape), _const_spec(wvt.shape), _const_spec(wg.shape)],
        out_specs=[tok(wdt), tr(a), tok(a), tr(a), tok(d), tok(d)],
        out_shape=[jax.ShapeDtypeStruct((b, s, wdt), BF16), jax.ShapeDtypeStruct((b, a, s), BF16),
                   jax.ShapeDtypeStruct((b, s, a), BF16), jax.ShapeDtypeStruct((b, a, s), BF16),
                   jax.ShapeDtypeStruct((b, s, d), BF16), jax.ShapeDtypeStruct((b, s, d), BF16)],
        compiler_params=_params(("parallel", "parallel")),
        name="in_proj",
    )(x, g, wu, wqt, wk, wvt, wg)


def _ssm_prep_kernel(lre_ref, lim_ref, ldt_ref, btr_ref, bti_ref, cr_ref, ci_ref,
                     taps_ref, wsr_ref, wsi_ref, wor_ref, woi_ref, ltr_ref, lti_ref):
    t = SSM_CHUNK
    lre = lre_ref[0]
    lim = lim_ref[0]
    dt = jnp.exp(ldt_ref[0])
    ea = lre * dt
    eb = lim * dt

    def powers(tau):
        mag = jnp.exp(tau * ea)
        return mag * jnp.cos(tau * eb), mag * jnp.sin(tau * eb)

    tau = lax.broadcasted_iota(jnp.int32, (t, 1), 0).astype(F32)
    p0r, p0i = powers(tau)
    p1r, p1i = powers(tau + 1.0)
    prr, pri = powers((t - 1.0) - tau)
    one = jnp.ones((1, 1), F32)
    lbr, lbi = powers(one)
    ltr, lti = powers(one * t)
    ltr_ref[0] = ltr
    lti_ref[0] = lti

    nr = lbr - 1.0
    den = lre * lre + lim * lim
    cfr = (nr * lre + lbi * lim) / den
    cfi = (lbi * lre - nr * lim) / den
    btr = btr_ref[0]
    bti = bti_ref[0]
    bbr = cfr * btr - cfi * bti
    bbi = cfr * bti + cfi * btr
    cr = cr_ref[0]
    ci = ci_ref[0]

    def scale_rows(pr, pi, mr, mi):
        re = [pr[s:s + 1, :] * mr - pi[s:s + 1, :] * mi for s in range(t)]
        im = [pr[s:s + 1, :] * mi + pi[s:s + 1, :] * mr for s in range(t)]
        return jnp.concatenate(re, axis=0), jnp.concatenate(im, axis=0)

    wsr, wsi = scale_rows(prr, pri, bbr, bbi)
    wsr_ref[0] = wsr.astype(BF16)
    wsi_ref[0] = wsi.astype(BF16)
    wor, woi = scale_rows(p1r, p1i, cr, ci)
    wor_ref[0] = wor.astype(BF16)
    woi_ref[0] = woi.astype(BF16)
    clr, cli = scale_rows(p0r, p0i, cr, ci)
    hi = lax.Precision.HIGHEST
    taps_ref[0] = (lax.dot_general(clr, bbr, _NT, precision=hi, preferred_element_type=F32)
                   - lax.dot_general(cli, bbi, _NT, precision=hi, preferred_element_type=F32))


def _ssm_prep(lam_re, lam_im, log_dt, b_re, b_im, c_re, c_im):
    g, p = lam_re.shape
    c = SSM_GROUP
    rows = SSM_CHUNK * c
    vec = pl.BlockSpec((1, 1, p), lambda gi: (gi, 0, 0))
    mat = pl.BlockSpec((1, c, p), lambda gi: (gi, 0, 0))
    big = pl.BlockSpec((1, rows, p), lambda gi: (gi, 0, 0))
    return pl.pallas_call(
        _ssm_prep_kernel,
        grid=(g,),
        in_specs=[vec, vec, pl.BlockSpec((1, 1, 1), lambda gi: (gi, 0, 0)), mat, mat, mat, mat],
        out_specs=[pl.BlockSpec((1, rows, c), lambda gi: (gi, 0, 0)), big, big, big, big, vec, vec],
        out_shape=[jax.ShapeDtypeStruct((g, rows, c), F32)]
        + [jax.ShapeDtypeStruct((g, rows, p), BF16)] * 4
        + [jax.ShapeDtypeStruct((g, 1, p), F32)] * 2,
        compiler_params=_params(("parallel",)),
        name="ssm_prep",
    )(lam_re.reshape(g, 1, p), lam_im.reshape(g, 1, p), log_dt.reshape(g, 1, 1),
      jnp.swapaxes(b_re, 1, 2), jnp.swapaxes(b_im, 1, 2), c_re, c_im)


def _toeplitz(taps):
    g = taps.shape[0]
    t, c = SSM_CHUNK, SSM_GROUP
    k4 = taps.reshape(g, t, c, c)
    lag = jnp.arange(t)[None, :] - jnp.arange(t)[:, None]
    blocks = k4[:, jnp.clip(lag, 0, t - 1)]
    blocks = jnp.where((lag >= 0)[None, :, :, None, None], blocks, 0.0)
    return blocks.transpose(0, 1, 4, 2, 3).reshape(g, t * c, t * c).astype(BF16)


def _s5_kernel(u_ref, m_ref, wsr_ref, wsi_ref, wor_ref, woi_ref, ltr_ref, lti_ref, d_ref,
               y_ref, hlr, hli, hpr, hpi, *, nb):
    u = u_ref[...]
    hlr[...] = jnp.dot(u, wsr_ref[0], preferred_element_type=F32)
    hli[...] = jnp.dot(u, wsi_ref[0], preferred_element_type=F32)
    ltr = ltr_ref[0]
    lti = lti_ref[0]
    n_chunks = u.shape[0] // nb

    def step(k, carry):
        sr, si = carry
        rows = pl.ds(pl.multiple_of(k * nb, nb), nb)
        hpr[rows, :] = sr
        hpi[rows, :] = si
        return (ltr * sr - lti * si + hlr[rows, :], ltr * si + lti * sr + hli[rows, :])

    zero = jnp.zeros((nb, SSM_STATE), F32)
    lax.fori_loop(0, n_chunks, step, (zero, zero))
    y = jnp.dot(u, m_ref[0], preferred_element_type=F32)
    y = y + lax.dot_general(hpr[...].astype(BF16), wor_ref[0], _NT, preferred_element_type=F32)
    y = y - lax.dot_general(hpi[...].astype(BF16), woi_ref[0], _NT, preferred_element_type=F32)
    y = y + d_ref[0] * u.astype(F32)
    y_ref[...] = jax.nn.gelu(y).astype(BF16)


def _s5(ut, toep, wsr, wsi, wor, woi, ltr, lti, dvec, nb):
    r, width = ut.shape
    g = toep.shape[0]
    w = width // g
    p = SSM_STATE
    grp = lambda shape: pl.BlockSpec((1,) + shape, lambda gi: (gi, 0, 0))
    col = pl.BlockSpec((r, w), lambda gi: (0, gi))
    return pl.pallas_call(
        functools.partial(_s5_kernel, nb=nb),
        grid=(g,),
        in_specs=[col, grp((w, w)), grp((w, p)), grp((w, p)), grp((w, p)), grp((w, p)),
                  grp((1, p)), grp((1, p)), grp((1, w))],
        out_specs=col,
        out_shape=jax.ShapeDtypeStruct((r, width), BF16),
        scratch_shapes=[pltpu.VMEM((r, p), F32)] * 4,
        compiler_params=_params(("parallel",)),
        name="s5",
    )(ut, toep, wsr, wsi, wor, woi, ltr, lti, dvec)


def _moba_kernel(qt_ref, k_ref, vt_ref, o_ref, *, n_blk, n_sel):
    blk = MOBA_BLOCK
    lanes = qt_ref.shape[1]
    hd = lanes // HEAD_PAIR
    gate_rows = 16
    means = [jnp.mean(k_ref[0, j * blk:(j + 1) * blk, :].astype(F32), axis=0, keepdims=True)
             for j in range(n_blk)]
    means.append(jnp.zeros((gate_rows - n_blk, lanes), F32))
    kmean = jnp.concatenate(means, axis=0).astype(BF16)
    head_row = lax.broadcasted_iota(jnp.int32, (lanes, blk), 0) // hd
    blk_row = lax.broadcasted_iota(jnp.int32, (gate_rows, blk), 0)
    causal = (lax.broadcasted_iota(jnp.int32, (blk, blk), 0)
              <= lax.broadcasted_iota(jnp.int32, (blk, blk), 1))

    for i in range(n_blk):
        qp = qt_ref[0, :, i * blk:(i + 1) * blk]
        nk = (i + 1) * blk
        outs = []
        for hh in range(HEAD_PAIR):
            qh = jnp.where(head_row == hh, qp, jnp.zeros_like(qp))
            s = jnp.dot(k_ref[0, 0:nk, :], qh, preferred_element_type=F32)
            bias = [None] * i
            if i > n_sel:
                gate = jnp.dot(kmean, qh, preferred_element_type=F32)
                for n in range(i):
                    gn = gate[n:n + 1, :]
                    ahead = (gate > gn) | ((gate == gn) & (blk_row < n))
                    ahead = ahead & (blk_row < i)
                    rank = jnp.sum(ahead.astype(F32), axis=0, keepdims=True)
                    bias[n] = jnp.where(rank < n_sel, 0.0, NEG_INF)
            parts = []
            tops = []
            for j in range(i + 1):
                sj = s[j * blk:(j + 1) * blk, :]
                if j == i:
                    sj = jnp.where(causal, sj, NEG_INF)
                top = jnp.max(sj, axis=0, keepdims=True)
                if j < i and bias[j] is not None:
                    top = top + bias[j]
                parts.append(sj)
                tops.append(top)
            m = functools.reduce(jnp.maximum, tops)
            den = jnp.zeros((1, blk), F32)
            probs = []
            for j in range(i + 1):
                shift = m - bias[j] if (j < i and bias[j] is not None) else m
                pj = jnp.exp(parts[j] - shift)
                den = den + jnp.sum(pj, axis=0, keepdims=True)
                probs.append(pj.astype(BF16))
            pt = jnp.concatenate(probs, axis=0) if i > 0 else probs[0]
            o = jnp.dot(vt_ref[0, hh * hd:(hh + 1) * hd, 0:nk], pt, preferred_element_type=F32)
            outs.append(o / den)
        ot = jnp.concatenate(outs, axis=0)
        o_ref[0, i * blk:(i + 1) * blk, :] = ot.T.astype(BF16)


def _moba(qt, k, vt):
    b, a, s = qt.shape
    lanes = HEAD_PAIR * (a // ATTN_HEADS)
    n_blk = s // MOBA_BLOCK
    tr = pl.BlockSpec((1, lanes, s), lambda bi, hi: (bi, hi, 0))
    tok = pl.BlockSpec((1, s, lanes), lambda bi, hi: (bi, 0, hi))
    return pl.pallas_call(
        functools.partial(_moba_kernel, n_blk=n_blk, n_sel=min(MOBA_TOPK, n_blk - 1)),
        grid=(b, a // lanes),
        in_specs=[tr, tok, tr],
        out_specs=tok,
        out_shape=jax.ShapeDtypeStruct((b, s, a), BF16),
        compiler_params=_params(("parallel", "parallel")),
        name="moba",
    )(qt, k, vt)


def _merge_kernel(yg_ref, at_ref, ga_ref, gb_ref, x_ref, wglu_ref, bglu_ref, wa_ref, wb_ref,
                  wo_ref, g_ref, o_ref):
    yg = yg_ref[...]
    glu = jnp.dot(yg, wglu_ref[...], preferred_element_type=F32) + bglu_ref[...]
    ya_in = (yg.astype(F32) * jax.nn.sigmoid(glu)).astype(BF16)
    ya = jnp.dot(ya_in, wa_ref[...], preferred_element_type=F32)
    yb = jnp.dot(at_ref[...], wb_ref[...], preferred_element_type=F32)
    mixed = (ga_ref[...].astype(F32) * ya + gb_ref[...].astype(F32) * yb).astype(BF16)
    mo = jnp.dot(mixed, wo_ref[...], preferred_element_type=F32)
    o_ref[...] = x_ref[...] + _rms(mo) * g_ref[...]


def _merge(yg, at, ga, gb, x, wglu, bglu, wa, wb, wo, g):
    n, d = x.shape
    tm = min(TOKEN_TILE, n)
    tok = lambda w: pl.BlockSpec((tm, w), lambda ti: (ti, 0))
    consts = [wglu, bglu, wa, wb, wo, g]
    return pl.pallas_call(
        _merge_kernel,
        grid=(n // tm,),
        in_specs=[tok(yg.shape[1]), tok(at.shape[1]), tok(d), tok(d), tok(d)]
        + [_const_spec(c.shape) for c in consts],
        out_specs=tok(d),
        out_shape=jax.ShapeDtypeStruct((n, d), F32),
        compiler_params=_params(("parallel",)),
        name="merge",
    )(yg, at, ga, gb, x, *consts)


def _mlp_ple_kernel(x_ref, p_ref, gpre_ref, w1_ref, w2_ref, gpost_ref, wple_ref, wpg_ref, gple_ref,
                    o_ref):
    x = x_ref[...]
    hm = (_rms(x) * gpre_ref[...]).astype(BF16)
    dff = w1_ref.shape[1]
    f = jnp.zeros(x.shape, F32)
    for c in range(0, dff, FF_CHUNK):
        hid = jnp.dot(hm, w1_ref[:, c:c + FF_CHUNK], preferred_element_type=F32)
        hid = jnp.square(jnp.maximum(hid, 0.0)).astype(BF16)
        f = f + jnp.dot(hid, w2_ref[c:c + FF_CHUNK, :], preferred_element_type=F32)
    x = x + _rms(f) * gpost_ref[...]
    e = jnp.dot(p_ref[...].astype(BF16), wple_ref[...], preferred_element_type=F32)
    e = e * jax.nn.sigmoid(jnp.dot(x.astype(BF16), wpg_ref[...], preferred_element_type=F32))
    o_ref[...] = x + _rms(e) * gple_ref[...]


def _mlp_ple(x, p, gpre, w1, w2, gpost, wple, wpg, gple):
    n, d = x.shape
    tm = min(TOKEN_TILE, n)
    tok = lambda w: pl.BlockSpec((tm, w), lambda ti: (ti, 0))
    consts = [gpre, w1, w2, gpost, wple, wpg, gple]
    return pl.pallas_call(
        _mlp_ple_kernel,
        grid=(n // tm,),
        in_specs=[tok(d), tok(p.shape[1])] + [_const_spec(c.shape) for c in consts],
        out_specs=tok(d),
        out_shape=jax.ShapeDtypeStruct((n, d), F32),
        compiler_params=_params(("parallel",)),
        name="mlp_ple",
    )(x, p, *consts)


def kernel(x, p, g_pre_mix, w_in, ssm_lam_re, ssm_lam_im, ssm_log_dt, ssm_b_re, ssm_b_im, ssm_c_re,
           ssm_c_im, ssm_d, w_glu, b_glu, w_branch_a, w_branch_b, w_out, g_post_mix, g_pre_mlp,
           w_mlp1, w_mlp2, g_post_mlp, w_ple, w_ple_gate, g_ple):
    b, s, d = x.shape
    depth = w_in.shape[0]
    wdt = w_glu.shape[1]
    a = w_branch_b.shape[1]
    groups = wdt // SSM_GROUP
    t = SSM_CHUNK
    n_chunks = s // t
    hd = a // ATTN_HEADS
    row = lambda v: v.reshape(1, -1)
    for i in range(depth):
        w = w_in[i].astype(BF16)
        o_q, o_k, o_v, o_g = wdt, wdt + a, wdt + 2 * a, wdt + 3 * a
        u, qt, k, vt, ga, gb = _in_proj(
            x, row(g_pre_mix[i]), w[:, :o_q], w[:, o_q:o_k].T, w[:, o_k:o_v], w[:, o_v:o_g].T,
            w[:, o_g:], float(hd) ** -0.5)

        taps, wsr, wsi, wor, woi, ltr, lti = _ssm_prep(
            ssm_lam_re[i], ssm_lam_im[i], ssm_log_dt[i], ssm_b_re[i], ssm_b_im[i], ssm_c_re[i],
            ssm_c_im[i])
        dvec = jnp.tile(ssm_d[i].reshape(groups, 1, SSM_GROUP), (1, t, 1)).reshape(groups, 1, -1)
        ut = u.reshape(b, n_chunks, t, groups, SSM_GROUP).transpose(1, 0, 3, 2, 4)
        yt = _s5(ut.reshape(n_chunks * b, -1), _toeplitz(taps), wsr, wsi, wor, woi, ltr, lti, dvec, b)
        yg = yt.reshape(n_chunks, b, groups, t, SSM_GROUP).transpose(1, 0, 3, 2, 4)

        at = _moba(qt, k, vt)

        n = b * s
        x1 = _merge(yg.reshape(n, wdt), at.reshape(n, a), ga.reshape(n, d), gb.reshape(n, d),
                    x.reshape(n, d), w_glu[i].astype(BF16), row(b_glu[i]),
                    w_branch_a[i].astype(BF16), w_branch_b[i].astype(BF16), w_out[i].astype(BF16),
                    row(g_post_mix[i]))
        x2 = _mlp_ple(x1, p[i].reshape(n, -1), row(g_pre_mlp[i]), w_mlp1[i].astype(BF16),
                      w_mlp2[i].astype(BF16), row(g_post_mlp[i]), w_ple[i].astype(BF16),
                      w_ple_gate[i].astype(BF16), row(g_ple[i]))
        x = x2.reshape(b, s, d)
    return x
```

```python
import functools

import jax
import jax.numpy as jnp
from jax import lax
from jax.experimental import pallas as pl
from jax.experimental.pallas import tpu as pltpu

F32 = jnp.float32
BF16 = jnp.bfloat16

SSM_GROUP = 16
SSM_STATE = 64
SSM_CHUNK = 8
LANES = 128
TILE_GROUPS = LANES // SSM_GROUP
SCAN_CHUNKS = 16
ATTN_HEADS = 8
MOBA_BLOCK = 256
MOBA_TOPK = 3
RMS_EPS = 1e-6
NEG_INF = -1e30
HEAD_PAIR = 2
TOKEN_TILE = 512
FF_CHUNK = 1024
VMEM_LIMIT = 48 * 1024 * 1024

_NT = (((1,), (1,)), ((), ()))


def _rms(v):
    return v * lax.rsqrt(jnp.mean(v * v, axis=-1, keepdims=True) + RMS_EPS)


def _const_spec(shape):
    nd = len(shape)
    return pl.BlockSpec(shape, lambda *_: (0,) * nd, pipeline_mode=pl.Buffered(1))


def _params(sem):
    return pltpu.CompilerParams(dimension_semantics=sem, vmem_limit_bytes=VMEM_LIMIT)


def _chunk_tile_spec(n_tiles, tm):
    return pl.BlockSpec((n_tiles, tm // SSM_CHUNK, None, SSM_CHUNK, LANES),
                        lambda bi, ti: (0, ti, bi, 0, 0))


def _in_proj_kernel(x_ref, g_ref, wu_ref, wqt_ref, wk_ref, wvt_ref, wg_ref,
                    u_ref, qt_ref, k_ref, vt_ref, ga_ref, gb_ref, *, q_scale):
    x = x_ref[0]
    hb = (_rms(x) * g_ref[...]).astype(BF16)
    tm, d = x.shape
    u = jnp.dot(hb, wu_ref[...], preferred_element_type=F32)
    for o in range(u_ref.shape[0]):
        u_ref[o] = u[:, o * LANES:(o + 1) * LANES].reshape(tm // SSM_CHUNK, SSM_CHUNK, LANES)
    k_ref[0] = jnp.dot(hb, wk_ref[...], preferred_element_type=F32).astype(BF16)
    qt = lax.dot_general(wqt_ref[...], hb, _NT, preferred_element_type=F32)
    qt_ref[0] = (qt * q_scale).astype(BF16)
    vt_ref[0] = lax.dot_general(wvt_ref[...], hb, _NT, preferred_element_type=F32).astype(BF16)
    gates = jax.nn.sigmoid(jnp.dot(hb, wg_ref[...], preferred_element_type=F32))
    ga_ref[0] = gates[:, :d].astype(BF16)
    gb_ref[0] = gates[:, d:].astype(BF16)


def _in_proj(x, g, wu, wqt, wk, wvt, wg, q_scale):
    b, s, d = x.shape
    n_tiles = wu.shape[1] // LANES
    a = wk.shape[1]
    tm = min(TOKEN_TILE, s)
    tok = lambda w: pl.BlockSpec((1, tm, w), lambda bi, ti: (bi, ti, 0))
    tr = lambda w: pl.BlockSpec((1, w, tm), lambda bi, ti: (bi, 0, ti))
    return pl.pallas_call(
        functools.partial(_in_proj_kernel, q_scale=q_scale),
        grid=(b, s // tm),
        in_specs=[tok(d), _const_spec(g.shape), _const_spec(wu.shape), _const_spec(wqt.shape),
                  _const_spec(wk.shape), _const_spec(wvt.shape), _const_spec(wg.shape)],
        out_specs=[_chunk_tile_spec(n_tiles, tm), tr(a), tok(a), tr(a), tok(d), tok(d)],
        out_shape=[jax.ShapeDtypeStruct((n_tiles, s // SSM_CHUNK, b, SSM_CHUNK, LANES), F32),
                   jax.ShapeDtypeStruct((b, a, s), BF16), jax.ShapeDtypeStruct((b, s, a), BF16),
                   jax.ShapeDtypeStruct((b, a, s), BF16),
                   jax.ShapeDtypeStruct((b, s, d), BF16), jax.ShapeDtypeStruct((b, s, d), BF16)],
        compiler_params=_params(("parallel", "parallel")),
        name="in_proj",
    )(x, g, wu, wqt, wk, wvt, wg)


def _ssm_prep_kernel(lre_ref, lim_ref, ldt_ref, btr_ref, bti_ref, cr_ref, ci_ref,
                     taps_ref, wsr_ref, wsi_ref, wor_ref, woi_ref, ltr_ref, lti_ref):
    t = SSM_CHUNK
    lre = lre_ref[0]
    lim = lim_ref[0]
    dt = jnp.exp(ldt_ref[0])
    ea = lre * dt
    eb = lim * dt

    def powers(tau):
        mag = jnp.exp(tau * ea)
        return mag * jnp.cos(tau * eb), mag * jnp.sin(tau * eb)

    tau = lax.broadcasted_iota(jnp.int32, (t, 1), 0).astype(F32)
    p0r, p0i = powers(tau)
    p1r, p1i = powers(tau + 1.0)
    prr, pri = powers((t - 1.0) - tau)
    one = jnp.ones((1, 1), F32)
    lbr, lbi = powers(one)
    ltr, lti = powers(one * t)
    ltr_ref[0] = ltr
    lti_ref[0] = lti

    nr = lbr - 1.0
    den = lre * lre + lim * lim
    cfr = (nr * lre + lbi * lim) / den
    cfi = (lbi * lre - nr * lim) / den
    btr = btr_ref[0]
    bti = bti_ref[0]
    bbr = cfr * btr - cfi * bti
    bbi = cfr * bti + cfi * btr
    cr = cr_ref[0]
    ci = ci_ref[0]

    def scale_rows(pr, pi, mr, mi):
        re = [pr[s:s + 1, :] * mr - pi[s:s + 1, :] * mi for s in range(t)]
        im = [pr[s:s + 1, :] * mi + pi[s:s + 1, :] * mr for s in range(t)]
        return jnp.concatenate(re, axis=0), jnp.concatenate(im, axis=0)

    wsr_ref[0], wsi_ref[0] = scale_rows(prr, pri, bbr, bbi)
    wor_ref[0], woi_ref[0] = scale_rows(p1r, p1i, cr, ci)
    clr, cli = scale_rows(p0r, p0i, cr, ci)
    hi = lax.Precision.HIGHEST
    taps_ref[0] = (lax.dot_general(clr, bbr, _NT, precision=hi, preferred_element_type=F32)
                   - lax.dot_general(cli, bbi, _NT, precision=hi, preferred_element_type=F32))


def _ssm_prep(lam_re, lam_im, log_dt, b_re, b_im, c_re, c_im):
    g, p = lam_re.shape
    c = SSM_GROUP
    rows = SSM_CHUNK * c
    vec = pl.BlockSpec((1, 1, p), lambda gi: (gi, 0, 0))
    mat = pl.BlockSpec((1, c, p), lambda gi: (gi, 0, 0))
    big = pl.BlockSpec((1, rows, p), lambda gi: (gi, 0, 0))
    return pl.pallas_call(
        _ssm_prep_kernel,
        grid=(g,),
        in_specs=[vec, vec, pl.BlockSpec((1, 1, 1), lambda gi: (gi, 0, 0)), mat, mat, mat, mat],
        out_specs=[pl.BlockSpec((1, rows, c), lambda gi: (gi, 0, 0)), big, big, big, big, vec, vec],
        out_shape=[jax.ShapeDtypeStruct((g, rows, c), F32)]
        + [jax.ShapeDtypeStruct((g, rows, p), F32)] * 4
        + [jax.ShapeDtypeStruct((g, 1, p), F32)] * 2,
        compiler_params=_params(("parallel",)),
        name="ssm_prep",
    )(lam_re.reshape(g, 1, p), lam_im.reshape(g, 1, p), log_dt.reshape(g, 1, 1),
      jnp.swapaxes(b_re, 1, 2), jnp.swapaxes(b_im, 1, 2), c_re, c_im)


def _tile_operators(taps, wsr, wsi, wor, woi, ltr, lti, d_skip):
    t, c, p, tg = SSM_CHUNK, SSM_GROUP, SSM_STATE, TILE_GROUPS
    n = taps.shape[0] // tg
    eye = jnp.eye(tg, dtype=F32)
    lag = jnp.arange(t)[None, :] - jnp.arange(t)[:, None]
    k5 = taps.reshape(n, tg, t, c, c)[:, :, jnp.clip(lag, 0, t - 1)]
    k5 = jnp.where((lag >= 0)[None, None, :, :, None, None], k5, 0.0)
    toep = jnp.einsum('ngstca,gh->nsgathc', k5, eye).reshape(n, t * LANES, t * LANES)

    def to_state(w):
        w5 = w.reshape(n, tg, t, c, p)
        return jnp.einsum('ngsap,gh->nsgahp', w5, eye).reshape(n, t * LANES, tg * p)

    def from_state(w):
        w5 = w.reshape(n, tg, t, c, p)
        return jnp.einsum('ngtcp,gh->ngpthc', w5, eye).reshape(n, tg * p, t * LANES)

    wst = jnp.concatenate([to_state(wsr), to_state(wsi)], axis=2)
    wout = jnp.concatenate([from_state(wor), -from_state(woi)], axis=1)
    lam_r = ltr.reshape(n, 1, tg * p)
    lam_i = lti.reshape(n, 1, tg * p)
    dvec = jnp.tile(d_skip.reshape(n, 1, LANES), (1, 1, t))
    return toep.astype(BF16), wst.astype(BF16), wout.astype(BF16), lam_r, lam_i, dvec


def _s5_kernel(u_ref, m_ref, wst_ref, wout_ref, lr_ref, li_ref, d_ref, y_ref, h_ref, st_ref, *, nb):
    t = SSM_CHUNK
    rows = u_ref.shape[0] // t
    half = lr_ref.shape[-1]

    @pl.when(pl.program_id(1) == 0)
    def _():
        st_ref[...] = jnp.zeros(st_ref.shape, F32)

    u32 = jnp.concatenate([u_ref[pl.ds(s, rows, stride=t), :] for s in range(t)], axis=1)
    u = u32.astype(BF16)
    h_ref[...] = jnp.dot(u, wst_ref[...], preferred_element_type=F32)
    lam_r = lr_ref[...]
    lam_i = li_ref[...]

    def step(k, carry):
        sr, si = carry
        rk = pl.ds(pl.multiple_of(k * nb, nb), nb)
        hr = h_ref[rk, :half]
        hi = h_ref[rk, half:]
        h_ref[rk, :half] = sr
        h_ref[rk, half:] = si
        return (lam_r * sr - lam_i * si + hr, lam_r * si + lam_i * sr + hi)

    sr, si = lax.fori_loop(0, rows // nb, step, (st_ref[0], st_ref[1]))
    st_ref[0] = sr
    st_ref[1] = si
    y = jnp.dot(u, m_ref[...], preferred_element_type=F32)
    y = y + jnp.dot(h_ref[...].astype(BF16), wout_ref[...], preferred_element_type=F32)
    y = jax.nn.gelu(y + d_ref[...] * u32)
    for s in range(t):
        y_ref[pl.ds(s, rows, stride=t), :] = y[:, s * LANES:(s + 1) * LANES]


def _s5(u5, toep, wst, wout, lam_r, lam_i, dvec):
    n_tiles, n_chunks, nb, t, lanes = u5.shape
    kb = min(SCAN_CHUNKS, n_chunks)
    blk_tokens = kb * nb * t
    flat = u5.reshape(n_tiles, n_chunks * nb * t, lanes)
    width = toep.shape[-1]
    st = wst.shape[-1]
    tok = pl.BlockSpec((None, blk_tokens, lanes), lambda oi, ki: (oi, ki, 0))
    op = lambda r, c: pl.BlockSpec((None, r, c), lambda oi, ki: (oi, 0, 0))
    y = pl.pallas_call(
        functools.partial(_s5_kernel, nb=nb),
        grid=(n_tiles, n_chunks // kb),
        in_specs=[tok, op(width, width), op(width, st), op(st, width), op(1, st // 2), op(1, st // 2),
                  op(1, width)],
        out_specs=tok,
        out_shape=jax.ShapeDtypeStruct(flat.shape, F32),
        scratch_shapes=[pltpu.VMEM((kb * nb, st), F32), pltpu.VMEM((2, nb, st // 2), F32)],
        compiler_params=_params(("parallel", "arbitrary")),
        name="s5",
    )(flat, toep, wst, wout, lam_r, lam_i, dvec)
    return y.reshape(u5.shape)


def _moba_kernel(qt_ref, k_ref, vt_ref, o_ref, *, n_blk, n_sel):
    blk = MOBA_BLOCK
    lanes = qt_ref.shape[1]
    hd = lanes // HEAD_PAIR
    gate_rows = 16
    means = [jnp.mean(k_ref[0, j * blk:(j + 1) * blk, :].astype(F32), axis=0, keepdims=True)
             for j in range(n_blk)]
    means.append(jnp.zeros((gate_rows - n_blk, lanes), F32))
    kmean = jnp.concatenate(means, axis=0).astype(BF16)
    head_row = lax.broadcasted_iota(jnp.int32, (lanes, blk), 0) // hd
    blk_row = lax.broadcasted_iota(jnp.int32, (gate_rows, blk), 0)
    causal = (lax.broadcasted_iota(jnp.int32, (blk, blk), 0)
              <= lax.broadcasted_iota(jnp.int32, (blk, blk), 1))

    for i in range(n_blk):
        qp = qt_ref[0, :, i * blk:(i + 1) * blk]
        nk = (i + 1) * blk
        outs = []
        for hh in range(HEAD_PAIR):
            qh = jnp.where(head_row == hh, qp, jnp.zeros_like(qp))
            s = jnp.dot(k_ref[0, 0:nk, :], qh, preferred_element_type=F32)
            bias = [None] * i
            if i > n_sel:
                gate = jnp.dot(kmean, qh, preferred_element_type=F32)
                for n in range(i):
                    gn = gate[n:n + 1, :]
                    ahead = (gate > gn) | ((gate == gn) & (blk_row < n))
                    ahead = ahead & (blk_row < i)
                    rank = jnp.sum(ahead.astype(F32), axis=0, keepdims=True)
                    bias[n] = jnp.where(rank < n_sel, 0.0, NEG_INF)
            parts = []
            tops = []
            for j in range(i + 1):
                sj = s[j * blk:(j + 1) * blk, :]
                if j == i:
                    sj = jnp.where(causal, sj, NEG_INF)
                top = jnp.max(sj, axis=0, keepdims=True)
                if j < i and bias[j] is not None:
                    top = top + bias[j]
                parts.append(sj)
                tops.append(top)
            m = functools.reduce(jnp.maximum, tops)
            den = jnp.zeros((1, blk), F32)
            probs = []
            for j in range(i + 1):
                shift = m - bias[j] if (j < i and bias[j] is not None) else m
                pj = jnp.exp(parts[j] - shift)
                den = den + jnp.sum(pj, axis=0, keepdims=True)
                probs.append(pj.astype(BF16))
            pt = jnp.concatenate(probs, axis=0) if i > 0 else probs[0]
            o = jnp.dot(vt_ref[0, hh * hd:(hh + 1) * hd, 0:nk], pt, preferred_element_type=F32)
            outs.append(o / den)
        ot = jnp.concatenate(outs, axis=0)
        o_ref[0, i * blk:(i + 1) * blk, :] = ot.T.astype(BF16)


def _moba(qt, k, vt):
    b, a, s = qt.shape
    lanes = HEAD_PAIR * (a // ATTN_HEADS)
    n_blk = s // MOBA_BLOCK
    tr = pl.BlockSpec((1, lanes, s), lambda bi, hi: (bi, hi, 0))
    tok = pl.BlockSpec((1, s, lanes), lambda bi, hi: (bi, 0, hi))
    return pl.pallas_call(
        functools.partial(_moba_kernel, n_blk=n_blk, n_sel=min(MOBA_TOPK, n_blk - 1)),
        grid=(b, a // lanes),
        in_specs=[tr, tok, tr],
        out_specs=tok,
        out_shape=jax.ShapeDtypeStruct((b, s, a), BF16),
        compiler_params=_params(("parallel", "parallel")),
        name="moba",
    )(qt, k, vt)


def _merge_kernel(yg_ref, at_ref, ga_ref, gb_ref, x_ref, wglu_ref, bglu_ref, wa_ref, wb_ref,
                  wo_ref, g_ref, o_ref):
    tm = x_ref.shape[1]
    yg = jnp.concatenate([yg_ref[o].reshape(tm, LANES) for o in range(yg_ref.shape[0])], axis=1)
    glu = jnp.dot(yg.astype(BF16), wglu_ref[...], preferred_element_type=F32) + bglu_ref[...]
    ya_in = (yg * jax.nn.sigmoid(glu)).astype(BF16)
    ya = jnp.dot(ya_in, wa_ref[...], preferred_element_type=F32)
    yb = jnp.dot(at_ref[0], wb_ref[...], preferred_element_type=F32)
    mixed = (ga_ref[0].astype(F32) * ya + gb_ref[0].astype(F32) * yb).astype(BF16)
    mo = jnp.dot(mixed, wo_ref[...], preferred_element_type=F32)
    o_ref[0] = x_ref[0] + _rms(mo) * g_ref[...]


def _merge(yg5, at, ga, gb, x, wglu, bglu, wa, wb, wo, g):
    b, s, d = x.shape
    tm = min(TOKEN_TILE, s)
    tok = lambda w: pl.BlockSpec((1, tm, w), lambda bi, ti: (bi, ti, 0))
    consts = [wglu, bglu, wa, wb, wo, g]
    return pl.pallas_call(
        _merge_kernel,
        grid=(b, s // tm),
        in_specs=[_chunk_tile_spec(yg5.shape[0], tm), tok(at.shape[2]), tok(d), tok(d), tok(d)]
        + [_const_spec(c.shape) for c in consts],
        out_specs=tok(d),
        out_shape=jax.ShapeDtypeStruct((b, s, d), F32),
        compiler_params=_params(("parallel", "parallel")),
        name="merge",
    )(yg5, at, ga, gb, x, *consts)


def _mlp_ple_kernel(x_ref, p_ref, gpre_ref, w1_ref, w2_ref, gpost_ref, wple_ref, wpg_ref, gple_ref,
                    o_ref):
    x = x_ref[...]
    hm = (_rms(x) * gpre_ref[...]).astype(BF16)
    dff = w1_ref.shape[1]
    f = jnp.zeros(x.shape, F32)
    for c in range(0, dff, FF_CHUNK):
        hid = jnp.dot(hm, w1_ref[:, c:c + FF_CHUNK], preferred_element_type=F32)
        hid = jnp.square(jnp.maximum(hid, 0.0)).astype(BF16)
        f = f + jnp.dot(hid, w2_ref[c:c + FF_CHUNK, :], preferred_element_type=F32)
    x = x + _rms(f) * gpost_ref[...]
    e = jnp.dot(p_ref[...].astype(BF16), wple_ref[...], preferred_element_type=F32)
    e = e * jax.nn.sigmoid(jnp.dot(x.astype(BF16), wpg_ref[...], preferred_element_type=F32))
    o_ref[...] = x + _rms(e) * gple_ref[...]


def _mlp_ple(x, p, gpre, w1, w2, gpost, wple, wpg, gple):
    n, d = x.shape
    tm = min(TOKEN_TILE, n)
    tok = lambda w: pl.BlockSpec((tm, w), lambda ti: (ti, 0))
    consts = [gpre, w1, w2, gpost, wple, wpg, gple]
    return pl.pallas_call(
        _mlp_ple_kernel,
        grid=(n // tm,),
        in_specs=[tok(d), tok(p.shape[1])] + [_const_spec(c.shape) for c in consts],
        out_specs=tok(d),
        out_shape=jax.ShapeDtypeStruct((n, d), F32),
        compiler_params=_params(("parallel",)),
        name="mlp_ple",
    )(x, p, *consts)


def kernel(x, p, g_pre_mix, w_in, ssm_lam_re, ssm_lam_im, ssm_log_dt, ssm_b_re, ssm_b_im, ssm_c_re,
           ssm_c_im, ssm_d, w_glu, b_glu, w_branch_a, w_branch_b, w_out, g_post_mix, g_pre_mlp,
           w_mlp1, w_mlp2, g_post_mlp, w_ple, w_ple_gate, g_ple):
    b, s, d = x.shape
    depth = w_in.shape[0]
    wdt = w_glu.shape[1]
    a = w_branch_b.shape[1]
    hd = a // ATTN_HEADS
    row = lambda v: v.reshape(1, -1)
    for i in range(depth):
        w = w_in[i].astype(BF16)
        o_q, o_k, o_v, o_g = wdt, wdt + a, wdt + 2 * a, wdt + 3 * a
        u5, qt, k, vt, ga, gb = _in_proj(
            x, row(g_pre_mix[i]), w[:, :o_q], w[:, o_q:o_k].T, w[:, o_k:o_v], w[:, o_v:o_g].T,
            w[:, o_g:], float(hd) ** -0.5)

        prep = _ssm_prep(ssm_lam_re[i], ssm_lam_im[i], ssm_log_dt[i], ssm_b_re[i], ssm_b_im[i],
                         ssm_c_re[i], ssm_c_im[i])
        yg5 = _s5(u5, *_tile_operators(*prep, ssm_d[i]))

        at = _moba(qt, k, vt)

        x1 = _merge(yg5, at, ga, gb, x, w_glu[i].astype(BF16), row(b_glu[i]),
                    w_branch_a[i].astype(BF16), w_branch_b[i].astype(BF16), w_out[i].astype(BF16),
                    row(g_post_mix[i]))
        n = b * s
        x2 = _mlp_ple(x1.reshape(n, d), p[i].reshape(n, -1), row(g_pre_mlp[i]),
                      w_mlp1[i].astype(BF16), w_mlp2[i].astype(BF16), row(g_post_mlp[i]),
                      w_ple[i].astype(BF16), w_ple_gate[i].astype(BF16), row(g_ple[i]))
        x = x2.reshape(b, s, d)
    return x
```

```python
import functools

import jax
import jax.numpy as jnp
from jax import lax
from jax.experimental import pallas as pl
from jax.experimental.pallas import tpu as pltpu

F32 = jnp.float32
BF16 = jnp.bfloat16

SSM_GROUP = 16
SSM_STATE = 64
SSM_CHUNK = 8
LANES = 128
TILE_GROUPS = LANES // SSM_GROUP
SCAN_CHUNKS = 16
ATTN_HEADS = 8
MOBA_BLOCK = 256
MOBA_TOPK = 3
RMS_EPS = 1e-6
NEG_INF = -1e30
LOG2E = 1.4426950408889634
HEAD_PAIR = 2
SUM_ROWS = 16
SCORE_LOOKAHEAD = 2
TOKEN_TILE = 512
FF_CHUNK = 1024
VMEM_LIMIT = 48 * 1024 * 1024

_NT = (((1,), (1,)), ((), ()))


def _rms(v):
    return v * lax.rsqrt(jnp.mean(v * v, axis=-1, keepdims=True) + RMS_EPS)


def _const_spec(shape):
    nd = len(shape)
    return pl.BlockSpec(shape, lambda *_: (0,) * nd, pipeline_mode=pl.Buffered(1))


def _params(sem):
    return pltpu.CompilerParams(dimension_semantics=sem, vmem_limit_bytes=VMEM_LIMIT)


def _chunk_tile_spec(n_tiles, tm):
    return pl.BlockSpec((n_tiles, tm // SSM_CHUNK, None, SSM_CHUNK, LANES),
                        lambda bi, ti: (0, ti, bi, 0, 0))


def _in_proj_kernel(x_ref, g_ref, wu_ref, wqt_ref, wk_ref, wvt_ref, wg_ref,
                    u_ref, qt_ref, k_ref, vt_ref, ga_ref, gb_ref, *, q_scale):
    x = x_ref[0]
    hb = (_rms(x) * g_ref[...]).astype(BF16)
    tm, d = x.shape
    u = jnp.dot(hb, wu_ref[...], preferred_element_type=F32)
    for o in range(u_ref.shape[0]):
        u_ref[o] = u[:, o * LANES:(o + 1) * LANES].reshape(tm // SSM_CHUNK, SSM_CHUNK, LANES)
    k_ref[0] = jnp.dot(hb, wk_ref[...], preferred_element_type=F32).astype(BF16)
    qt = lax.dot_general(wqt_ref[...], hb, _NT, preferred_element_type=F32)
    qt_ref[0] = (qt * q_scale).astype(BF16)
    vt_ref[0] = lax.dot_general(wvt_ref[...], hb, _NT, preferred_element_type=F32).astype(BF16)
    gates = jax.nn.sigmoid(jnp.dot(hb, wg_ref[...], preferred_element_type=F32))
    ga_ref[0] = gates[:, :d].astype(BF16)
    gb_ref[0] = gates[:, d:].astype(BF16)


def _in_proj(x, g, wu, wqt, wk, wvt, wg, q_scale):
    b, s, d = x.shape
    n_tiles = wu.shape[1] // LANES
    a = wk.shape[1]
    tm = min(TOKEN_TILE, s)
    tok = lambda w: pl.BlockSpec((1, tm, w), lambda bi, ti: (bi, ti, 0))
    tr = lambda w: pl.BlockSpec((1, w, tm), lambda bi, ti: (bi, 0, ti))
    return pl.pallas_call(
        functools.partial(_in_proj_kernel, q_scale=q_scale),
        grid=(b, s // tm),
        in_specs=[tok(d), _const_spec(g.shape), _const_spec(wu.shape), _const_spec(wqt.shape),
                  _const_spec(wk.shape), _const_spec(wvt.shape), _const_spec(wg.shape)],
        out_specs=[_chunk_tile_spec(n_tiles, tm), tr(a), tok(a), tr(a), tok(d), tok(d)],
        out_shape=[jax.ShapeDtypeStruct((n_tiles, s // SSM_CHUNK, b, SSM_CHUNK, LANES), F32),
                   jax.ShapeDtypeStruct((b, a, s), BF16), jax.ShapeDtypeStruct((b, s, a), BF16),
                   jax.ShapeDtypeStruct((b, a, s), BF16),
                   jax.ShapeDtypeStruct((b, s, d), BF16), jax.ShapeDtypeStruct((b, s, d), BF16)],
        compiler_params=_params(("parallel", "parallel")),
        name="in_proj",
    )(x, g, wu, wqt, wk, wvt, wg)


def _ssm_prep_kernel(lre_ref, lim_ref, ldt_ref, btr_ref, bti_ref, cr_ref, ci_ref,
                     taps_ref, wsr_ref, wsi_ref, wor_ref, woi_ref, ltr_ref, lti_ref):
    t = SSM_CHUNK
    lre = lre_ref[0]
    lim = lim_ref[0]
    dt = jnp.exp(ldt_ref[0])
    ea = lre * dt
    eb = lim * dt

    def powers(tau):
        mag = jnp.exp(tau * ea)
        return mag * jnp.cos(tau * eb), mag * jnp.sin(tau * eb)

    tau = lax.broadcasted_iota(jnp.int32, (t, 1), 0).astype(F32)
    p0r, p0i = powers(tau)
    p1r, p1i = powers(tau + 1.0)
    prr, pri = powers((t - 1.0) - tau)
    one = jnp.ones((1, 1), F32)
    lbr, lbi = powers(one)
    ltr, lti = powers(one * t)
    ltr_ref[0] = ltr
    lti_ref[0] = lti

    nr = lbr - 1.0
    den = lre * lre + lim * lim
    cfr = (nr * lre + lbi * lim) / den
    cfi = (lbi * lre - nr * lim) / den
    btr = btr_ref[0]
    bti = bti_ref[0]
    bbr = cfr * btr - cfi * bti
    bbi = cfr * bti + cfi * btr
    cr = cr_ref[0]
    ci = ci_ref[0]

    def scale_rows(pr, pi, mr, mi):
        re = [pr[s:s + 1, :] * mr - pi[s:s + 1, :] * mi for s in range(t)]
        im = [pr[s:s + 1, :] * mi + pi[s:s + 1, :] * mr for s in range(t)]
        return jnp.concatenate(re, axis=0), jnp.concatenate(im, axis=0)

    wsr_ref[0], wsi_ref[0] = scale_rows(prr, pri, bbr, bbi)
    wor_ref[0], woi_ref[0] = scale_rows(p1r, p1i, cr, ci)
    clr, cli = scale_rows(p0r, p0i, cr, ci)
    hi = lax.Precision.HIGHEST
    taps_ref[0] = (lax.dot_general(clr, bbr, _NT, precision=hi, preferred_element_type=F32)
                   - lax.dot_general(cli, bbi, _NT, precision=hi, preferred_element_type=F32))


def _ssm_prep(lam_re, lam_im, log_dt, b_re, b_im, c_re, c_im):
    g, p = lam_re.shape
    c = SSM_GROUP
    rows = SSM_CHUNK * c
    vec = pl.BlockSpec((1, 1, p), lambda gi: (gi, 0, 0))
    mat = pl.BlockSpec((1, c, p), lambda gi: (gi, 0, 0))
    big = pl.BlockSpec((1, rows, p), lambda gi: (gi, 0, 0))
    return pl.pallas_call(
        _ssm_prep_kernel,
        grid=(g,),
        in_specs=[vec, vec, pl.BlockSpec((1, 1, 1), lambda gi: (gi, 0, 0)), mat, mat, mat, mat],
        out_specs=[pl.BlockSpec((1, rows, c), lambda gi: (gi, 0, 0)), big, big, big, big, vec, vec],
        out_shape=[jax.ShapeDtypeStruct((g, rows, c), F32)]
        + [jax.ShapeDtypeStruct((g, rows, p), F32)] * 4
        + [jax.ShapeDtypeStruct((g, 1, p), F32)] * 2,
        compiler_params=_params(("parallel",)),
        name="ssm_prep",
    )(lam_re.reshape(g, 1, p), lam_im.reshape(g, 1, p), log_dt.reshape(g, 1, 1),
      jnp.swapaxes(b_re, 1, 2), jnp.swapaxes(b_im, 1, 2), c_re, c_im)


def _tile_operators(taps, wsr, wsi, wor, woi, ltr, lti, d_skip):
    t, c, p, tg = SSM_CHUNK, SSM_GROUP, SSM_STATE, TILE_GROUPS
    n = taps.shape[0] // tg
    eye = jnp.eye(tg, dtype=F32)
    lag = jnp.arange(t)[None, :] - jnp.arange(t)[:, None]
    k5 = taps.reshape(n, tg, t, c, c)[:, :, jnp.clip(lag, 0, t - 1)]
    k5 = jnp.where((lag >= 0)[None, None, :, :, None, None], k5, 0.0)
    toep = jnp.einsum('ngstca,gh->nsgathc', k5, eye).reshape(n, t * LANES, t * LANES)

    def to_state(w):
        w5 = w.reshape(n, tg, t, c, p)
        return jnp.einsum('ngsap,gh->nsgahp', w5, eye).reshape(n, t * LANES, tg * p)

    def from_state(w):
        w5 = w.reshape(n, tg, t, c, p)
        return jnp.einsum('ngtcp,gh->ngpthc', w5, eye).reshape(n, tg * p, t * LANES)

    wst = jnp.concatenate([to_state(wsr), to_state(wsi)], axis=2)
    wout = jnp.concatenate([from_state(wor), -from_state(woi)], axis=1)
    lam_r = ltr.reshape(n, 1, tg * p)
    lam_i = lti.reshape(n, 1, tg * p)
    dvec = jnp.tile(d_skip.reshape(n, 1, LANES), (1, 1, t))
    return toep.astype(BF16), wst.astype(BF16), wout.astype(BF16), lam_r, lam_i, dvec


def _s5_kernel(u_ref, m_ref, wst_ref, wout_ref, lr_ref, li_ref, d_ref, y_ref, h_ref, st_ref, *, nb):
    t = SSM_CHUNK
    rows = u_ref.shape[0] // t
    half = lr_ref.shape[-1]

    @pl.when(pl.program_id(1) == 0)
    def _():
        st_ref[...] = jnp.zeros(st_ref.shape, F32)

    u32 = jnp.concatenate([u_ref[pl.ds(s, rows, stride=t), :] for s in range(t)], axis=1)
    u = u32.astype(BF16)
    h_ref[...] = jnp.dot(u, wst_ref[...], preferred_element_type=F32)
    lam_r = lr_ref[...]
    lam_i = li_ref[...]

    def step(k, carry):
        sr, si = carry
        rk = pl.ds(pl.multiple_of(k * nb, nb), nb)
        hr = h_ref[rk, :half]
        hi = h_ref[rk, half:]
        h_ref[rk, :half] = sr
        h_ref[rk, half:] = si
        return (lam_r * sr - lam_i * si + hr, lam_r * si + lam_i * sr + hi)

    sr, si = lax.fori_loop(0, rows // nb, step, (st_ref[0], st_ref[1]))
    st_ref[0] = sr
    st_ref[1] = si
    y = jnp.dot(u, m_ref[...], preferred_element_type=F32)
    y = y + jnp.dot(h_ref[...].astype(BF16), wout_ref[...], preferred_element_type=F32)
    y = jax.nn.gelu(y + d_ref[...] * u32)
    for s in range(t):
        y_ref[pl.ds(s, rows, stride=t), :] = y[:, s * LANES:(s + 1) * LANES]


def _s5(u5, toep, wst, wout, lam_r, lam_i, dvec):
    n_tiles, n_chunks, nb, t, lanes = u5.shape
    kb = min(SCAN_CHUNKS, n_chunks)
    blk_tokens = kb * nb * t
    flat = u5.reshape(n_tiles, n_chunks * nb * t, lanes)
    width = toep.shape[-1]
    st = wst.shape[-1]
    tok = pl.BlockSpec((None, blk_tokens, lanes), lambda oi, ki: (oi, ki, 0))
    op = lambda r, c: pl.BlockSpec((None, r, c), lambda oi, ki: (oi, 0, 0))
    y = pl.pallas_call(
        functools.partial(_s5_kernel, nb=nb),
        grid=(n_tiles, n_chunks // kb),
        in_specs=[tok, op(width, width), op(width, st), op(st, width), op(1, st // 2), op(1, st // 2),
                  op(1, width)],
        out_specs=tok,
        out_shape=jax.ShapeDtypeStruct(flat.shape, F32),
        scratch_shapes=[pltpu.VMEM((kb * nb, st), F32), pltpu.VMEM((2, nb, st // 2), F32)],
        compiler_params=_params(("parallel", "arbitrary")),
        name="s5",
    )(flat, toep, wst, wout, lam_r, lam_i, dvec)
    return y.reshape(u5.shape)


def _moba_kernel(qt_ref, k_ref, vt_ref, o_ref, *, n_blk, n_sel):
    blk = MOBA_BLOCK
    lanes = qt_ref.shape[1]
    hd = lanes // HEAD_PAIR
    gate_rows = 16
    means = [jnp.mean(k_ref[0, j * blk:(j + 1) * blk, :].astype(F32), axis=0, keepdims=True)
             for j in range(n_blk)]
    means.append(jnp.zeros((gate_rows - n_blk, lanes), F32))
    kmean = jnp.concatenate(means, axis=0).astype(BF16)
    head_row = lax.broadcasted_iota(jnp.int32, (lanes, blk), 0) // hd
    blk_row = lax.broadcasted_iota(jnp.int32, (gate_rows, blk), 0)
    causal = (lax.broadcasted_iota(jnp.int32, (blk, blk), 0)
              <= lax.broadcasted_iota(jnp.int32, (blk, blk), 1))
    ones = jnp.ones((SUM_ROWS, k_ref.shape[1]), BF16)
    v_ext = [jnp.concatenate([vt_ref[0, hh * hd:(hh + 1) * hd, :], ones], axis=0)
             for hh in range(HEAD_PAIR)]

    def scores(i, hh):
        qp = qt_ref[0, :, i * blk:(i + 1) * blk]
        qh = jnp.where(head_row == hh, qp, jnp.zeros_like(qp))
        s = jnp.dot(k_ref[0, 0:(i + 1) * blk, :], qh, preferred_element_type=F32)
        bias = [None] * i
        if i > n_sel:
            gate = jnp.dot(kmean, qh, preferred_element_type=F32)
            for n in range(i):
                gn = gate[n:n + 1, :]
                ahead = (gate > gn) | ((gate == gn) & (blk_row < n))
                ahead = ahead & (blk_row < i)
                rank = jnp.sum(ahead.astype(F32), axis=0, keepdims=True)
                bias[n] = jnp.where(rank < n_sel, 0.0, NEG_INF)
        return s, bias

    def attend(i, hh, s, bias):
        parts = []
        tops = []
        for j in range(i + 1):
            sj = s[j * blk:(j + 1) * blk, :]
            if j == i:
                sj = jnp.where(causal, sj, NEG_INF)
            top = jnp.max(sj, axis=0, keepdims=True)
            if j < i and bias[j] is not None:
                top = top + bias[j]
            parts.append(sj)
            tops.append(top)
        m = functools.reduce(jnp.maximum, tops)
        probs = []
        for j in range(i + 1):
            shift = m - bias[j] if (j < i and bias[j] is not None) else m
            probs.append(jnp.exp2(parts[j] - shift).astype(BF16))
        pt = jnp.concatenate(probs, axis=0) if i > 0 else probs[0]
        o = jnp.dot(v_ext[hh][:, 0:(i + 1) * blk], pt, preferred_element_type=F32)
        return o[:hd] / o[hd:hd + 1]

    units = [(i, hh) for i in reversed(range(n_blk)) for hh in range(HEAD_PAIR)]
    pending = [scores(*u) for u in units[:SCORE_LOOKAHEAD]]
    outs = []
    for n, (i, hh) in enumerate(units):
        cur = pending.pop(0)
        if n + SCORE_LOOKAHEAD < len(units):
            pending.append(scores(*units[n + SCORE_LOOKAHEAD]))
        outs.append(attend(i, hh, *cur))
        if hh == HEAD_PAIR - 1:
            ot = jnp.concatenate(outs, axis=0)
            o_ref[0, i * blk:(i + 1) * blk, :] = ot.T.astype(BF16)
            outs = []


def _moba(qt, k, vt):
    b, a, s = qt.shape
    lanes = HEAD_PAIR * (a // ATTN_HEADS)
    n_blk = s // MOBA_BLOCK
    tr = pl.BlockSpec((1, lanes, s), lambda bi, hi: (bi, hi, 0))
    tok = pl.BlockSpec((1, s, lanes), lambda bi, hi: (bi, 0, hi))
    return pl.pallas_call(
        functools.partial(_moba_kernel, n_blk=n_blk, n_sel=min(MOBA_TOPK, n_blk - 1)),
        grid=(b, a // lanes),
        in_specs=[tr, tok, tr],
        out_specs=tok,
        out_shape=jax.ShapeDtypeStruct((b, s, a), BF16),
        compiler_params=_params(("parallel", "parallel")),
        name="moba",
    )(qt, k, vt)


def _merge_kernel(yg_ref, at_ref, ga_ref, gb_ref, x_ref, wglu_ref, bglu_ref, wa_ref, wb_ref,
                  wo_ref, g_ref, o_ref):
    tm = x_ref.shape[1]
    yg = jnp.concatenate([yg_ref[o].reshape(tm, LANES) for o in range(yg_ref.shape[0])], axis=1)
    glu = jnp.dot(yg.astype(BF16), wglu_ref[...], preferred_element_type=F32) + bglu_ref[...]
    ya_in = (yg * jax.nn.sigmoid(glu)).astype(BF16)
    ya = jnp.dot(ya_in, wa_ref[...], preferred_element_type=F32)
    yb = jnp.dot(at_ref[0], wb_ref[...], preferred_element_type=F32)
    mixed = (ga_ref[0].astype(F32) * ya + gb_ref[0].astype(F32) * yb).astype(BF16)
    mo = jnp.dot(mixed, wo_ref[...], preferred_element_type=F32)
    o_ref[0] = x_ref[0] + _rms(mo) * g_ref[...]


def _merge(yg5, at, ga, gb, x, wglu, bglu, wa, wb, wo, g):
    b, s, d = x.shape
    tm = min(TOKEN_TILE, s)
    tok = lambda w: pl.BlockSpec((1, tm, w), lambda bi, ti: (bi, ti, 0))
    consts = [wglu, bglu, wa, wb, wo, g]
    return pl.pallas_call(
        _merge_kernel,
        grid=(b, s // tm),
        in_specs=[_chunk_tile_spec(yg5.shape[0], tm), tok(at.shape[2]), tok(d), tok(d), tok(d)]
        + [_const_spec(c.shape) for c in consts],
        out_specs=tok(d),
        out_shape=jax.ShapeDtypeStruct((b, s, d), F32),
        compiler_params=_params(("parallel", "parallel")),
        name="merge",
    )(yg5, at, ga, gb, x, *consts)


def _mlp_ple_kernel(x_ref, p_ref, gpre_ref, w1_ref, w2_ref, gpost_ref, wple_ref, wpg_ref, gple_ref,
                    o_ref):
    x = x_ref[...]
    hm = (_rms(x) * gpre_ref[...]).astype(BF16)
    dff = w1_ref.shape[1]
    f = jnp.zeros(x.shape, F32)
    for c in range(0, dff, FF_CHUNK):
        hid = jnp.dot(hm, w1_ref[:, c:c + FF_CHUNK], preferred_element_type=F32)
        hid = jnp.square(jnp.maximum(hid, 0.0)).astype(BF16)
        f = f + jnp.dot(hid, w2_ref[c:c + FF_CHUNK, :], preferred_element_type=F32)
    x = x + _rms(f) * gpost_ref[...]
    e = jnp.dot(p_ref[...].astype(BF16), wple_ref[...], preferred_element_type=F32)
    e = e * jax.nn.sigmoid(jnp.dot(x.astype(BF16), wpg_ref[...], preferred_element_type=F32))
    o_ref[...] = x + _rms(e) * gple_ref[...]


def _mlp_ple(x, p, gpre, w1, w2, gpost, wple, wpg, gple):
    n, d = x.shape
    tm = min(TOKEN_TILE, n)
    tok = lambda w: pl.BlockSpec((tm, w), lambda ti: (ti, 0))
    consts = [gpre, w1, w2, gpost, wple, wpg, gple]
    return pl.pallas_call(
        _mlp_ple_kernel,
        grid=(n // tm,),
        in_specs=[tok(d), tok(p.shape[1])] + [_const_spec(c.shape) for c in consts],
        out_specs=tok(d),
        out_shape=jax.ShapeDtypeStruct((n, d), F32),
        compiler_params=_params(("parallel",)),
        name="mlp_ple",
    )(x, p, *consts)


def kernel(x, p, g_pre_mix, w_in, ssm_lam_re, ssm_lam_im, ssm_log_dt, ssm_b_re, ssm_b_im, ssm_c_re,
           ssm_c_im, ssm_d, w_glu, b_glu, w_branch_a, w_branch_b, w_out, g_post_mix, g_pre_mlp,
           w_mlp1, w_mlp2, g_post_mlp, w_ple, w_ple_gate, g_ple):
    b, s, d = x.shape
    depth = w_in.shape[0]
    wdt = w_glu.shape[1]
    a = w_branch_b.shape[1]
    hd = a // ATTN_HEADS
    row = lambda v: v.reshape(1, -1)
    for i in range(depth):
        w = w_in[i].astype(BF16)
        o_q, o_k, o_v, o_g = wdt, wdt + a, wdt + 2 * a, wdt + 3 * a
        u5, qt, k, vt, ga, gb = _in_proj(
            x, row(g_pre_mix[i]), w[:, :o_q], w[:, o_q:o_k].T, w[:, o_k:o_v], w[:, o_v:o_g].T,
            w[:, o_g:], float(hd) ** -0.5 * LOG2E)

        prep = _ssm_prep(ssm_lam_re[i], ssm_lam_im[i], ssm_log_dt[i], ssm_b_re[i], ssm_b_im[i],
                         ssm_c_re[i], ssm_c_im[i])
        yg5 = _s5(u5, *_tile_operators(*prep, ssm_d[i]))

        at = _moba(qt, k, vt)

        x1 = _merge(yg5, at, ga, gb, x, w_glu[i].astype(BF16), row(b_glu[i]),
                    w_branch_a[i].astype(BF16), w_branch_b[i].astype(BF16), w_out[i].astype(BF16),
                    row(g_post_mix[i]))
        n = b * s
        x2 = _mlp_ple(x1.reshape(n, d), p[i].reshape(n, -1), row(g_pre_mlp[i]),
                      w_mlp1[i].astype(BF16), w_mlp2[i].astype(BF16), row(g_post_mlp[i]),
                      w_ple[i].astype(BF16), w_ple_gate[i].astype(BF16), row(g_ple[i]))
        x = x2.reshape(b, s, d)
    return x
```

```python
import functools

import jax
import jax.numpy as jnp
from jax import lax
from jax.experimental import pallas as pl
from jax.experimental.pallas import tpu as pltpu

F32 = jnp.float32
BF16 = jnp.bfloat16

SSM_GROUP = 16
SSM_STATE = 64
SSM_CHUNK = 8
LANES = 128
TILE_GROUPS = LANES // SSM_GROUP
SCAN_CHUNKS = 16
ATTN_HEADS = 8
MOBA_BLOCK = 256
MOBA_TOPK = 3
RMS_EPS = 1e-6
NEG_INF = -1e30
LOG2E = 1.4426950408889634
HEAD_PAIR = 2
SUM_ROWS = 16
SCORE_LOOKAHEAD = 2
TOKEN_TILE = 512
FF_CHUNK = 1024
VMEM_LIMIT = 48 * 1024 * 1024

_NT = (((1,), (1,)), ((), ()))


def _rms(v):
    return v * lax.rsqrt(jnp.mean(v * v, axis=-1, keepdims=True) + RMS_EPS)


def _const_spec(shape):
    nd = len(shape)
    return pl.BlockSpec(shape, lambda *_: (0,) * nd, pipeline_mode=pl.Buffered(1))


def _params(sem):
    return pltpu.CompilerParams(dimension_semantics=sem, vmem_limit_bytes=VMEM_LIMIT)


def _chunk_tile_spec(n_tiles, tm):
    return pl.BlockSpec((n_tiles, tm // SSM_CHUNK, None, SSM_CHUNK, LANES),
                        lambda bi, ti: (0, ti, bi, 0, 0))


def _in_proj_kernel(x_ref, g_ref, wu_ref, wqt_ref, wk_ref, wvt_ref, wg_ref,
                    u_ref, qt_ref, k_ref, vt_ref, ga_ref, gb_ref, *, q_scale):
    x = x_ref[0]
    hb = (_rms(x) * g_ref[...]).astype(BF16)
    tm, d = x.shape
    u = jnp.dot(hb, wu_ref[...], preferred_element_type=F32)
    for o in range(u_ref.shape[0]):
        u_ref[o] = u[:, o * LANES:(o + 1) * LANES].reshape(tm // SSM_CHUNK, SSM_CHUNK, LANES)
    k_ref[0] = jnp.dot(hb, wk_ref[...], preferred_element_type=F32).astype(BF16)
    qt = lax.dot_general(wqt_ref[...], hb, _NT, preferred_element_type=F32)
    qt_ref[0] = (qt * q_scale).astype(BF16)
    vt_ref[0] = lax.dot_general(wvt_ref[...], hb, _NT, preferred_element_type=F32).astype(BF16)
    gates = jax.nn.sigmoid(jnp.dot(hb, wg_ref[...], preferred_element_type=F32))
    ga_ref[0] = gates[:, :d].astype(BF16)
    gb_ref[0] = gates[:, d:].astype(BF16)


def _in_proj(x, g, wu, wqt, wk, wvt, wg, q_scale):
    b, s, d = x.shape
    n_tiles = wu.shape[1] // LANES
    a = wk.shape[1]
    tm = min(TOKEN_TILE, s)
    tok = lambda w: pl.BlockSpec((1, tm, w), lambda bi, ti: (bi, ti, 0))
    tr = lambda w: pl.BlockSpec((1, w, tm), lambda bi, ti: (bi, 0, ti))
    return pl.pallas_call(
        functools.partial(_in_proj_kernel, q_scale=q_scale),
        grid=(b, s // tm),
        in_specs=[tok(d), _const_spec(g.shape), _const_spec(wu.shape), _const_spec(wqt.shape),
                  _const_spec(wk.shape), _const_spec(wvt.shape), _const_spec(wg.shape)],
        out_specs=[_chunk_tile_spec(n_tiles, tm), tr(a), tok(a), tr(a), tok(d), tok(d)],
        out_shape=[jax.ShapeDtypeStruct((n_tiles, s // SSM_CHUNK, b, SSM_CHUNK, LANES), F32),
                   jax.ShapeDtypeStruct((b, a, s), BF16), jax.ShapeDtypeStruct((b, s, a), BF16),
                   jax.ShapeDtypeStruct((b, a, s), BF16),
                   jax.ShapeDtypeStruct((b, s, d), BF16), jax.ShapeDtypeStruct((b, s, d), BF16)],
        compiler_params=_params(("parallel", "parallel")),
        name="in_proj",
    )(x, g, wu, wqt, wk, wvt, wg)


def _ssm_prep_kernel(lre_ref, lim_ref, ldt_ref, btr_ref, bti_ref, cr_ref, ci_ref,
                     toep_ref, wst_ref, wout_ref, ltr_ref, lti_ref):
    t = SSM_CHUNK
    rows, width = cr_ref.shape[1], cr_ref.shape[2]
    lre = lre_ref[0]
    lim = lim_ref[0]
    dt = jnp.exp(ldt_ref[0])
    ea = lre * dt
    eb = lim * dt

    def power(tau):
        mag = jnp.exp(tau * ea)
        return mag * jnp.cos(tau * eb), mag * jnp.sin(tau * eb)

    def times(pw, mr, mi):
        return pw[0] * mr - pw[1] * mi, pw[0] * mi + pw[1] * mr

    pw = [power(float(tau)) for tau in range(t + 1)]
    ltr_ref[0], lti_ref[0] = pw[t]
    same_group = (lax.broadcasted_iota(jnp.int32, (rows, width), 0) // SSM_GROUP
                  == lax.broadcasted_iota(jnp.int32, (rows, width), 1) // SSM_STATE)
    keep = lambda ref: jnp.where(same_group, ref[0], 0.0)
    cr, ci = keep(cr_ref), keep(ci_ref)
    nr = pw[1][0] - 1.0
    ni = pw[1][1]
    den = lre * lre + lim * lim
    coef = ((nr * lre + ni * lim) / den, (ni * lre - nr * lim) / den)
    bbr, bbi = times(coef, keep(btr_ref), keep(bti_ref))

    for s in range(t):
        wr, wi = times(pw[t - 1 - s], bbr, bbi)
        wst_ref[0, s * rows:(s + 1) * rows, :] = jnp.concatenate([wr, wi], axis=1).astype(BF16)
        orr, oi = times(pw[s + 1], cr, ci)
        wout_ref[0, :width, s * rows:(s + 1) * rows] = orr.T.astype(BF16)
        wout_ref[0, width:, s * rows:(s + 1) * rows] = (-oi).T.astype(BF16)

    cl = [times(pw[tau], cr, ci) for tau in range(t)]
    clr = jnp.concatenate([m[0] for m in cl], axis=0)
    cli = jnp.concatenate([m[1] for m in cl], axis=0)
    hi = lax.Precision.HIGHEST
    bd = (lax.dot_general(bbr, clr, _NT, precision=hi, preferred_element_type=F32)
          - lax.dot_general(bbi, cli, _NT, precision=hi, preferred_element_type=F32)).astype(BF16)
    for s in range(t):
        lead = [jnp.zeros((rows, s * rows), BF16)] if s else []
        toep_ref[0, s * rows:(s + 1) * rows, :] = jnp.concatenate(
            lead + [bd[:, :(t - s) * rows]], axis=1)


def _ssm_prep(lam_re, lam_im, log_dt, b_re, b_im, c_re, c_im):
    g, p = lam_re.shape
    tg, t = TILE_GROUPS, SSM_CHUNK
    n = g // tg
    width = tg * p
    lanes = lambda v: v.reshape(n, 1, width)
    chan = lambda m: jnp.tile(m.reshape(n, LANES, p), (1, 1, tg))
    vec = pl.BlockSpec((1, 1, width), lambda ti: (ti, 0, 0))
    mat = pl.BlockSpec((1, LANES, width), lambda ti: (ti, 0, 0))
    sq = lambda r, c: pl.BlockSpec((1, r, c), lambda ti: (ti, 0, 0))
    return pl.pallas_call(
        _ssm_prep_kernel,
        grid=(n,),
        in_specs=[vec, vec, vec, mat, mat, mat, mat],
        out_specs=[sq(t * LANES, t * LANES), sq(t * LANES, 2 * width), sq(2 * width, t * LANES),
                   vec, vec],
        out_shape=[jax.ShapeDtypeStruct((n, t * LANES, t * LANES), BF16),
                   jax.ShapeDtypeStruct((n, t * LANES, 2 * width), BF16),
                   jax.ShapeDtypeStruct((n, 2 * width, t * LANES), BF16),
                   jax.ShapeDtypeStruct((n, 1, width), F32), jax.ShapeDtypeStruct((n, 1, width), F32)],
        compiler_params=_params(("parallel",)),
        name="ssm_prep",
    )(lanes(lam_re), lanes(lam_im), lanes(jnp.repeat(log_dt, p)),
      chan(jnp.swapaxes(b_re, 1, 2)), chan(jnp.swapaxes(b_im, 1, 2)), chan(c_re), chan(c_im))


def _s5_kernel(u_ref, m_ref, wst_ref, wout_ref, lr_ref, li_ref, d_ref, y_ref, h_ref, st_ref, *, nb):
    t = SSM_CHUNK
    rows = u_ref.shape[0] // t
    half = lr_ref.shape[-1]

    @pl.when(pl.program_id(1) == 0)
    def _():
        st_ref[...] = jnp.zeros(st_ref.shape, F32)

    u32 = jnp.concatenate([u_ref[pl.ds(s, rows, stride=t), :] for s in range(t)], axis=1)
    u = u32.astype(BF16)
    h_ref[...] = jnp.dot(u, wst_ref[...], preferred_element_type=F32)
    lam_r = lr_ref[...]
    lam_i = li_ref[...]

    def step(k, carry):
        sr, si = carry
        rk = pl.ds(pl.multiple_of(k * nb, nb), nb)
        hr = h_ref[rk, :half]
        hi = h_ref[rk, half:]
        h_ref[rk, :half] = sr
        h_ref[rk, half:] = si
        return (lam_r * sr - lam_i * si + hr, lam_r * si + lam_i * sr + hi)

    sr, si = lax.fori_loop(0, rows // nb, step, (st_ref[0], st_ref[1]))
    st_ref[0] = sr
    st_ref[1] = si
    y = jnp.dot(u, m_ref[...], preferred_element_type=F32)
    y = y + jnp.dot(h_ref[...].astype(BF16), wout_ref[...], preferred_element_type=F32)
    y = jax.nn.gelu(y + d_ref[...] * u32)
    for s in range(t):
        y_ref[pl.ds(s, rows, stride=t), :] = y[:, s * LANES:(s + 1) * LANES]


def _s5(u5, toep, wst, wout, lam_r, lam_i, dvec):
    n_tiles, n_chunks, nb, t, lanes = u5.shape
    kb = min(SCAN_CHUNKS, n_chunks)
    blk_tokens = kb * nb * t
    flat = u5.reshape(n_tiles, n_chunks * nb * t, lanes)
    width = toep.shape[-1]
    st = wst.shape[-1]
    tok = pl.BlockSpec((None, blk_tokens, lanes), lambda oi, ki: (oi, ki, 0))
    op = lambda r, c: pl.BlockSpec((None, r, c), lambda oi, ki: (oi, 0, 0))
    y = pl.pallas_call(
        functools.partial(_s5_kernel, nb=nb),
        grid=(n_tiles, n_chunks // kb),
        in_specs=[tok, op(width, width), op(width, st), op(st, width), op(1, st // 2), op(1, st // 2),
                  op(1, width)],
        out_specs=tok,
        out_shape=jax.ShapeDtypeStruct(flat.shape, F32),
        scratch_shapes=[pltpu.VMEM((kb * nb, st), F32), pltpu.VMEM((2, nb, st // 2), F32)],
        compiler_params=_params(("parallel", "arbitrary")),
        name="s5",
    )(flat, toep, wst, wout, lam_r, lam_i, dvec)
    return y.reshape(u5.shape)


def _moba_kernel(qt_ref, k_ref, vt_ref, o_ref, *, n_blk, n_sel):
    blk = MOBA_BLOCK
    lanes = qt_ref.shape[1]
    hd = lanes // HEAD_PAIR
    gate_rows = 16
    means = [jnp.mean(k_ref[0, j * blk:(j + 1) * blk, :].astype(F32), axis=0, keepdims=True)
             for j in range(n_blk)]
    means.append(jnp.zeros((gate_rows - n_blk, lanes), F32))
    kmean = jnp.concatenate(means, axis=0).astype(BF16)
    head_row = lax.broadcasted_iota(jnp.int32, (lanes, blk), 0) // hd
    blk_row = lax.broadcasted_iota(jnp.int32, (gate_rows, blk), 0)
    causal = (lax.broadcasted_iota(jnp.int32, (blk, blk), 0)
              <= lax.broadcasted_iota(jnp.int32, (blk, blk), 1))
    ones = jnp.ones((SUM_ROWS, k_ref.shape[1]), BF16)
    v_ext = [jnp.concatenate([vt_ref[0, hh * hd:(hh + 1) * hd, :], ones], axis=0)
             for hh in range(HEAD_PAIR)]

    def scores(i, hh):
        qp = qt_ref[0, :, i * blk:(i + 1) * blk]
        qh = jnp.where(head_row == hh, qp, jnp.zeros_like(qp))
        s = jnp.dot(k_ref[0, 0:(i + 1) * blk, :], qh, preferred_element_type=F32)
        bias = [None] * i
        if i > n_sel:
            gate = jnp.dot(kmean, qh, preferred_element_type=F32)
            for n in range(i):
                gn = gate[n:n + 1, :]
                ahead = (gate > gn) | ((gate == gn) & (blk_row < n))
                ahead = ahead & (blk_row < i)
                rank = jnp.sum(ahead.astype(F32), axis=0, keepdims=True)
                bias[n] = jnp.where(rank < n_sel, 0.0, NEG_INF)
        return s, bias

    def attend(i, hh, s, bias):
        parts = []
        tops = []
        for j in range(i + 1):
            sj = s[j * blk:(j + 1) * blk, :]
            if j == i:
                sj = jnp.where(causal, sj, NEG_INF)
            top = jnp.max(sj, axis=0, keepdims=True)
            if j < i and bias[j] is not None:
                top = top + bias[j]
            parts.append(sj)
            tops.append(top)
        m = functools.reduce(jnp.maximum, tops)
        probs = []
        for j in range(i + 1):
            shift = m - bias[j] if (j < i and bias[j] is not None) else m
            probs.append(jnp.exp2(parts[j] - shift).astype(BF16))
        pt = jnp.concatenate(probs, axis=0) if i > 0 else probs[0]
        o = jnp.dot(v_ext[hh][:, 0:(i + 1) * blk], pt, preferred_element_type=F32)
        return o[:hd] / o[hd:hd + 1]

    units = [(i, hh) for i in reversed(range(n_blk)) for hh in range(HEAD_PAIR)]
    pending = [scores(*u) for u in units[:SCORE_LOOKAHEAD]]
    outs = []
    for n, (i, hh) in enumerate(units):
        cur = pending.pop(0)
        if n + SCORE_LOOKAHEAD < len(units):
            pending.append(scores(*units[n + SCORE_LOOKAHEAD]))
        outs.append(attend(i, hh, *cur))
        if hh == HEAD_PAIR - 1:
            ot = jnp.concatenate(outs, axis=0)
            o_ref[0, i * blk:(i + 1) * blk, :] = ot.T.astype(BF16)
            outs = []


def _moba(qt, k, vt):
    b, a, s = qt.shape
    lanes = HEAD_PAIR * (a // ATTN_HEADS)
    n_blk = s // MOBA_BLOCK
    tr = pl.BlockSpec((1, lanes, s), lambda bi, hi: (bi, hi, 0))
    tok = pl.BlockSpec((1, s, lanes), lambda bi, hi: (bi, 0, hi))
    return pl.pallas_call(
        functools.partial(_moba_kernel, n_blk=n_blk, n_sel=min(MOBA_TOPK, n_blk - 1)),
        grid=(b, a // lanes),
        in_specs=[tr, tok, tr],
        out_specs=tok,
        out_shape=jax.ShapeDtypeStruct((b, s, a), BF16),
        compiler_params=_params(("parallel", "parallel")),
        name="moba",
    )(qt, k, vt)


def _merge_kernel(yg_ref, at_ref, ga_ref, gb_ref, x_ref, wglu_ref, bglu_ref, wa_ref, wb_ref,
                  wo_ref, g_ref, o_ref):
    tm = x_ref.shape[1]
    yg = jnp.concatenate([yg_ref[o].reshape(tm, LANES) for o in range(yg_ref.shape[0])], axis=1)
    glu = jnp.dot(yg.astype(BF16), wglu_ref[...], preferred_element_type=F32) + bglu_ref[...]
    ya_in = (yg * jax.nn.sigmoid(glu)).astype(BF16)
    ya = jnp.dot(ya_in, wa_ref[...], preferred_element_type=F32)
    yb = jnp.dot(at_ref[0], wb_ref[...], preferred_element_type=F32)
    mixed = (ga_ref[0].astype(F32) * ya + gb_ref[0].astype(F32) * yb).astype(BF16)
    mo = jnp.dot(mixed, wo_ref[...], preferred_element_type=F32)
    o_ref[0] = x_ref[0] + _rms(mo) * g_ref[...]


def _merge(yg5, at, ga, gb, x, wglu, bglu, wa, wb, wo, g):
    b, s, d = x.shape
    tm = min(TOKEN_TILE, s)
    tok = lambda w: pl.BlockSpec((1, tm, w), lambda bi, ti: (bi, ti, 0))
    consts = [wglu, bglu, wa, wb, wo, g]
    return pl.pallas_call(
        _merge_kernel,
        grid=(b, s // tm),
        in_specs=[_chunk_tile_spec(yg5.shape[0], tm), tok(at.shape[2]), tok(d), tok(d), tok(d)]
        + [_const_spec(c.shape) for c in consts],
        out_specs=tok(d),
        out_shape=jax.ShapeDtypeStruct((b, s, d), F32),
        compiler_params=_params(("parallel", "parallel")),
        name="merge",
    )(yg5, at, ga, gb, x, *consts)


def _mlp_ple_kernel(x_ref, p_ref, gpre_ref, w1_ref, w2_ref, gpost_ref, wple_ref, wpg_ref, gple_ref,
                    o_ref):
    x = x_ref[...]
    hm = (_rms(x) * gpre_ref[...]).astype(BF16)
    dff = w1_ref.shape[1]
    f = jnp.zeros(x.shape, F32)
    for c in range(0, dff, FF_CHUNK):
        hid = jnp.dot(hm, w1_ref[:, c:c + FF_CHUNK], preferred_element_type=F32)
        hid = jnp.square(jnp.maximum(hid, 0.0)).astype(BF16)
        f = f + jnp.dot(hid, w2_ref[c:c + FF_CHUNK, :], preferred_element_type=F32)
    x = x + _rms(f) * gpost_ref[...]
    e = jnp.dot(p_ref[...].astype(BF16), wple_ref[...], preferred_element_type=F32)
    e = e * jax.nn.sigmoid(jnp.dot(x.astype(BF16), wpg_ref[...], preferred_element_type=F32))
    o_ref[...] = x + _rms(e) * gple_ref[...]


def _mlp_ple(x, p, gpre, w1, w2, gpost, wple, wpg, gple):
    n, d = x.shape
    tm = min(TOKEN_TILE, n)
    tok = lambda w: pl.BlockSpec((tm, w), lambda ti: (ti, 0))
    consts = [gpre, w1, w2, gpost, wple, wpg, gple]
    return pl.pallas_call(
        _mlp_ple_kernel,
        grid=(n // tm,),
        in_specs=[tok(d), tok(p.shape[1])] + [_const_spec(c.shape) for c in consts],
        out_specs=tok(d),
        out_shape=jax.ShapeDtypeStruct((n, d), F32),
        compiler_params=_params(("parallel",)),
        name="mlp_ple",
    )(x, p, *consts)


def kernel(x, p, g_pre_mix, w_in, ssm_lam_re, ssm_lam_im, ssm_log_dt, ssm_b_re, ssm_b_im, ssm_c_re,
           ssm_c_im, ssm_d, w_glu, b_glu, w_branch_a, w_branch_b, w_out, g_post_mix, g_pre_mlp,
           w_mlp1, w_mlp2, g_post_mlp, w_ple, w_ple_gate, g_ple):
    b, s, d = x.shape
    depth = w_in.shape[0]
    wdt = w_glu.shape[1]
    a = w_branch_b.shape[1]
    hd = a // ATTN_HEADS
    row = lambda v: v.reshape(1, -1)
    for i in range(depth):
        w = w_in[i].astype(BF16)
        o_q, o_k, o_v, o_g = wdt, wdt + a, wdt + 2 * a, wdt + 3 * a
        u5, qt, k, vt, ga, gb = _in_proj(
            x, row(g_pre_mix[i]), w[:, :o_q], w[:, o_q:o_k].T, w[:, o_k:o_v], w[:, o_v:o_g].T,
            w[:, o_g:], float(hd) ** -0.5 * LOG2E)

        prep = _ssm_prep(ssm_lam_re[i], ssm_lam_im[i], ssm_log_dt[i], ssm_b_re[i], ssm_b_im[i],
                         ssm_c_re[i], ssm_c_im[i])
        dvec = jnp.tile(ssm_d[i].reshape(-1, 1, LANES), (1, 1, SSM_CHUNK))
        yg5 = _s5(u5, *prep, dvec)

        at = _moba(qt, k, vt)

        x1 = _merge(yg5, at, ga, gb, x, w_glu[i].astype(BF16), row(b_glu[i]),
                    w_branch_a[i].astype(BF16), w_branch_b[i].astype(BF16), w_out[i].astype(BF16),
                    row(g_post_mix[i]))
        n = b * s
        x2 = _mlp_ple(x1.reshape(n, d), p[i].reshape(n, -1), row(g_pre_mlp[i]),
                      w_mlp1[i].astype(BF16), w_mlp2[i].astype(BF16), row(g_post_mlp[i]),
                      w_ple[i].astype(BF16), w_ple_gate[i].astype(BF16), row(g_ple[i]))
        x = x2.reshape(b, s, d)
    return x
```

```python
import functools

import jax
import jax.numpy as jnp
from jax import lax
from jax.experimental import pallas as pl
from jax.experimental.pallas import tpu as pltpu

F32 = jnp.float32
BF16 = jnp.bfloat16

SSM_GROUP = 16
SSM_STATE = 64
SSM_CHUNK = 8
LANES = 128
TILE_GROUPS = LANES // SSM_GROUP
SCAN_CHUNKS = 32
ATTN_HEADS = 8
MOBA_BLOCK = 256
MOBA_TOPK = 3
RMS_EPS = 1e-6
NEG_INF = -1e30
LOG2E = 1.4426950408889634
HEAD_PAIR = 2
SUM_ROWS = 16
SCORE_LOOKAHEAD = 2
TOKEN_TILE = 1024
FF_CHUNK = 1024
VMEM_LIMIT = 48 * 1024 * 1024

_NT = (((1,), (1,)), ((), ()))


def _rms(v):
    return v * lax.rsqrt(jnp.mean(v * v, axis=-1, keepdims=True) + RMS_EPS)


def _const_spec(shape):
    nd = len(shape)
    return pl.BlockSpec(shape, lambda *_: (0,) * nd, pipeline_mode=pl.Buffered(1))


def _params(sem):
    return pltpu.CompilerParams(dimension_semantics=sem, vmem_limit_bytes=VMEM_LIMIT)


def _chunk_tile_spec(n_tiles, tm):
    return pl.BlockSpec((n_tiles, tm // SSM_CHUNK, None, SSM_CHUNK, LANES),
                        lambda bi, ti: (0, ti, bi, 0, 0))


def _in_proj_kernel(x_ref, g_ref, wu_ref, wqt_ref, wk_ref, wvt_ref, wg_ref,
                    u_ref, qt_ref, k_ref, vt_ref, ga_ref, gb_ref, *, q_scale):
    x = x_ref[0]
    hb = (_rms(x) * g_ref[...]).astype(BF16)
    tm, d = x.shape
    u = jnp.dot(hb, wu_ref[...], preferred_element_type=F32)
    for o in range(u_ref.shape[0]):
        u_ref[o] = u[:, o * LANES:(o + 1) * LANES].reshape(tm // SSM_CHUNK, SSM_CHUNK, LANES)
    k_ref[0] = jnp.dot(hb, wk_ref[...], preferred_element_type=F32).astype(BF16)
    qt = lax.dot_general(wqt_ref[...], hb, _NT, preferred_element_type=F32)
    qt_ref[0] = (qt * q_scale).astype(BF16)
    vt_ref[0] = lax.dot_general(wvt_ref[...], hb, _NT, preferred_element_type=F32).astype(BF16)
    gates = jax.nn.sigmoid(jnp.dot(hb, wg_ref[...], preferred_element_type=F32))
    ga_ref[0] = gates[:, :d].astype(BF16)
    gb_ref[0] = gates[:, d:].astype(BF16)


def _in_proj(x, g, wu, wqt, wk, wvt, wg, q_scale):
    b, s, d = x.shape
    n_tiles = wu.shape[1] // LANES
    a = wk.shape[1]
    tm = min(TOKEN_TILE, s)
    tok = lambda w: pl.BlockSpec((1, tm, w), lambda bi, ti: (bi, ti, 0))
    tr = lambda w: pl.BlockSpec((1, w, tm), lambda bi, ti: (bi, 0, ti))
    return pl.pallas_call(
        functools.partial(_in_proj_kernel, q_scale=q_scale),
        grid=(b, s // tm),
        in_specs=[tok(d), _const_spec(g.shape), _const_spec(wu.shape), _const_spec(wqt.shape),
                  _const_spec(wk.shape), _const_spec(wvt.shape), _const_spec(wg.shape)],
        out_specs=[_chunk_tile_spec(n_tiles, tm), tr(a), tok(a), tr(a), tok(d), tok(d)],
        out_shape=[jax.ShapeDtypeStruct((n_tiles, s // SSM_CHUNK, b, SSM_CHUNK, LANES), F32),
                   jax.ShapeDtypeStruct((b, a, s), BF16), jax.ShapeDtypeStruct((b, s, a), BF16),
                   jax.ShapeDtypeStruct((b, a, s), BF16),
                   jax.ShapeDtypeStruct((b, s, d), BF16), jax.ShapeDtypeStruct((b, s, d), BF16)],
        compiler_params=_params(("parallel", "parallel")),
        name="in_proj",
    )(x, g, wu, wqt, wk, wvt, wg)


def _ssm_prep_kernel(lre_ref, lim_ref, ldt_ref, btr_ref, bti_ref, cr_ref, ci_ref,
                     toep_ref, wst_ref, wout_ref, ltr_ref, lti_ref):
    t = SSM_CHUNK
    rows, width = cr_ref.shape[1], cr_ref.shape[2]
    lre = lre_ref[0]
    lim = lim_ref[0]
    dt = jnp.exp(ldt_ref[0])
    ea = lre * dt
    eb = lim * dt

    def power(tau):
        mag = jnp.exp(tau * ea)
        return mag * jnp.cos(tau * eb), mag * jnp.sin(tau * eb)

    def times(pw, mr, mi):
        return pw[0] * mr - pw[1] * mi, pw[0] * mi + pw[1] * mr

    pw = [power(float(tau)) for tau in range(t + 1)]
    ltr_ref[0], lti_ref[0] = pw[t]
    same_group = (lax.broadcasted_iota(jnp.int32, (rows, width), 0) // SSM_GROUP
                  == lax.broadcasted_iota(jnp.int32, (rows, width), 1) // SSM_STATE)
    keep = lambda ref: jnp.where(same_group, ref[0], 0.0)
    cr, ci = keep(cr_ref), keep(ci_ref)
    nr = pw[1][0] - 1.0
    ni = pw[1][1]
    den = lre * lre + lim * lim
    coef = ((nr * lre + ni * lim) / den, (ni * lre - nr * lim) / den)
    bbr, bbi = times(coef, keep(btr_ref), keep(bti_ref))

    for s in range(t):
        wr, wi = times(pw[t - 1 - s], bbr, bbi)
        wst_ref[0, s * rows:(s + 1) * rows, :] = jnp.concatenate([wr, wi], axis=1).astype(BF16)
        orr, oi = times(pw[s + 1], cr, ci)
        wout_ref[0, :width, s * rows:(s + 1) * rows] = orr.T.astype(BF16)
        wout_ref[0, width:, s * rows:(s + 1) * rows] = (-oi).T.astype(BF16)

    cl = [times(pw[tau], cr, ci) for tau in range(t)]
    clr = jnp.concatenate([m[0] for m in cl], axis=0)
    cli = jnp.concatenate([m[1] for m in cl], axis=0)
    hi = lax.Precision.HIGHEST
    bd = (lax.dot_general(bbr, clr, _NT, precision=hi, preferred_element_type=F32)
          - lax.dot_general(bbi, cli, _NT, precision=hi, preferred_element_type=F32)).astype(BF16)
    for s in range(t):
        lead = [jnp.zeros((rows, s * rows), BF16)] if s else []
        toep_ref[0, s * rows:(s + 1) * rows, :] = jnp.concatenate(
            lead + [bd[:, :(t - s) * rows]], axis=1)


def _ssm_prep(lam_re, lam_im, log_dt, b_re, b_im, c_re, c_im):
    g, p = lam_re.shape
    tg, t = TILE_GROUPS, SSM_CHUNK
    n = g // tg
    width = tg * p
    lanes = lambda v: v.reshape(n, 1, width)
    chan = lambda m: jnp.tile(m.reshape(n, LANES, p), (1, 1, tg))
    vec = pl.BlockSpec((1, 1, width), lambda ti: (ti, 0, 0))
    mat = pl.BlockSpec((1, LANES, width), lambda ti: (ti, 0, 0))
    sq = lambda r, c: pl.BlockSpec((1, r, c), lambda ti: (ti, 0, 0))
    return pl.pallas_call(
        _ssm_prep_kernel,
        grid=(n,),
        in_specs=[vec, vec, vec, mat, mat, mat, mat],
        out_specs=[sq(t * LANES, t * LANES), sq(t * LANES, 2 * width), sq(2 * width, t * LANES),
                   vec, vec],
        out_shape=[jax.ShapeDtypeStruct((n, t * LANES, t * LANES), BF16),
                   jax.ShapeDtypeStruct((n, t * LANES, 2 * width), BF16),
                   jax.ShapeDtypeStruct((n, 2 * width, t * LANES), BF16),
                   jax.ShapeDtypeStruct((n, 1, width), F32), jax.ShapeDtypeStruct((n, 1, width), F32)],
        compiler_params=_params(("parallel",)),
        name="ssm_prep",
    )(lanes(lam_re), lanes(lam_im), lanes(jnp.repeat(log_dt, p)),
      chan(jnp.swapaxes(b_re, 1, 2)), chan(jnp.swapaxes(b_im, 1, 2)), chan(c_re), chan(c_im))


def _s5_kernel(u_ref, m_ref, wst_ref, wout_ref, lr_ref, li_ref, d_ref, y_ref, h_ref, st_ref, *, nb):
    t = SSM_CHUNK
    rows = u_ref.shape[0] // t
    half = lr_ref.shape[-1]

    @pl.when(pl.program_id(1) == 0)
    def _():
        st_ref[...] = jnp.zeros(st_ref.shape, F32)

    u32 = jnp.concatenate([u_ref[pl.ds(s, rows, stride=t), :] for s in range(t)], axis=1)
    u = u32.astype(BF16)
    h_ref[...] = jnp.dot(u, wst_ref[...], preferred_element_type=F32)
    lam_r = lr_ref[...]
    lam_i = li_ref[...]

    def step(k, carry):
        sr, si = carry
        rk = pl.ds(pl.multiple_of(k * nb, nb), nb)
        hr = h_ref[rk, :half]
        hi = h_ref[rk, half:]
        h_ref[rk, :half] = sr
        h_ref[rk, half:] = si
        return (lam_r * sr - lam_i * si + hr, lam_r * si + lam_i * sr + hi)

    y = jnp.dot(u, m_ref[...], preferred_element_type=F32) + d_ref[...] * u32
    sr, si = lax.fori_loop(0, rows // nb, step, (st_ref[0], st_ref[1]), unroll=True)
    st_ref[0] = sr
    st_ref[1] = si
    y = y + jnp.dot(h_ref[...].astype(BF16), wout_ref[...], preferred_element_type=F32)
    y = jax.nn.gelu(y)
    for s in range(t):
        y_ref[pl.ds(s, rows, stride=t), :] = y[:, s * LANES:(s + 1) * LANES]


def _s5(u5, toep, wst, wout, lam_r, lam_i, dvec):
    n_tiles, n_chunks, nb, t, lanes = u5.shape
    kb = min(SCAN_CHUNKS, n_chunks)
    blk_tokens = kb * nb * t
    flat = u5.reshape(n_tiles, n_chunks * nb * t, lanes)
    width = toep.shape[-1]
    st = wst.shape[-1]
    tok = pl.BlockSpec((None, blk_tokens, lanes), lambda oi, ki: (oi, ki, 0))
    op = lambda r, c: pl.BlockSpec((None, r, c), lambda oi, ki: (oi, 0, 0))
    y = pl.pallas_call(
        functools.partial(_s5_kernel, nb=nb),
        grid=(n_tiles, n_chunks // kb),
        in_specs=[tok, op(width, width), op(width, st), op(st, width), op(1, st // 2), op(1, st // 2),
                  op(1, width)],
        out_specs=tok,
        out_shape=jax.ShapeDtypeStruct(flat.shape, F32),
        scratch_shapes=[pltpu.VMEM((kb * nb, st), F32), pltpu.VMEM((2, nb, st // 2), F32)],
        compiler_params=_params(("parallel", "arbitrary")),
        name="s5",
    )(flat, toep, wst, wout, lam_r, lam_i, dvec)
    return y.reshape(u5.shape)


def _moba_kernel(qt_ref, k_ref, vt_ref, o_ref, *, n_blk, n_sel):
    blk = MOBA_BLOCK
    lanes = qt_ref.shape[1]
    hd = lanes // HEAD_PAIR
    gate_rows = 16
    means = [jnp.mean(k_ref[0, j * blk:(j + 1) * blk, :].astype(F32), axis=0, keepdims=True)
             for j in range(n_blk)]
    means.append(jnp.zeros((gate_rows - n_blk, lanes), F32))
    kmean = jnp.concatenate(means, axis=0).astype(BF16)
    head_row = lax.broadcasted_iota(jnp.int32, (lanes, blk), 0) // hd
    blk_row = lax.broadcasted_iota(jnp.int32, (gate_rows, blk), 0)
    causal = (lax.broadcasted_iota(jnp.int32, (blk, blk), 0)
              <= lax.broadcasted_iota(jnp.int32, (blk, blk), 1))
    ones = jnp.ones((SUM_ROWS, k_ref.shape[1]), BF16)
    v_ext = [jnp.concatenate([vt_ref[0, hh * hd:(hh + 1) * hd, :], ones], axis=0)
             for hh in range(HEAD_PAIR)]

    def scores(i, hh):
        qp = qt_ref[0, :, i * blk:(i + 1) * blk]
        qh = jnp.where(head_row == hh, qp, jnp.zeros_like(qp))
        s = jnp.dot(k_ref[0, 0:(i + 1) * blk, :], qh, preferred_element_type=F32)
        bias = [None] * i
        if i > n_sel:
            gate = jnp.dot(kmean, qh, preferred_element_type=F32)
            for n in range(i):
                gn = gate[n:n + 1, :]
                ahead = (gate > gn) | ((gate == gn) & (blk_row < n))
                ahead = ahead & (blk_row < i)
                rank = jnp.sum(ahead.astype(F32), axis=0, keepdims=True)
                bias[n] = jnp.where(rank < n_sel, 0.0, NEG_INF)
        return s, bias

    def attend(i, hh, s, bias):
        parts = []
        tops = []
        for j in range(i + 1):
            sj = s[j * blk:(j + 1) * blk, :]
            if j == i:
                sj = jnp.where(causal, sj, NEG_INF)
            top = jnp.max(sj, axis=0, keepdims=True)
            if j < i and bias[j] is not None:
                top = top + bias[j]
            parts.append(sj)
            tops.append(top)
        m = functools.reduce(jnp.maximum, tops)
        probs = []
        for j in range(i + 1):
            shift = m - bias[j] if (j < i and bias[j] is not None) else m
            probs.append(jnp.exp2(parts[j] - shift).astype(BF16))
        pt = jnp.concatenate(probs, axis=0) if i > 0 else probs[0]
        o = jnp.dot(v_ext[hh][:, 0:(i + 1) * blk], pt, preferred_element_type=F32)
        return o[:hd] / o[hd:hd + 1]

    units = [(i, hh) for i in reversed(range(n_blk)) for hh in range(HEAD_PAIR)]
    pending = [scores(*u) for u in units[:SCORE_LOOKAHEAD]]
    outs = []
    for n, (i, hh) in enumerate(units):
        cur = pending.pop(0)
        if n + SCORE_LOOKAHEAD < len(units):
            pending.append(scores(*units[n + SCORE_LOOKAHEAD]))
        outs.append(attend(i, hh, *cur))
        if hh == HEAD_PAIR - 1:
            ot = jnp.concatenate(outs, axis=0)
            o_ref[0, i * blk:(i + 1) * blk, :] = ot.T.astype(BF16)
            outs = []


def _moba(qt, k, vt):
    b, a, s = qt.shape
    lanes = HEAD_PAIR * (a // ATTN_HEADS)
    n_blk = s // MOBA_BLOCK
    tr = pl.BlockSpec((1, lanes, s), lambda bi, hi: (bi, hi, 0))
    tok = pl.BlockSpec((1, s, lanes), lambda bi, hi: (bi, 0, hi))
    return pl.pallas_call(
        functools.partial(_moba_kernel, n_blk=n_blk, n_sel=min(MOBA_TOPK, n_blk - 1)),
        grid=(b, a // lanes),
        in_specs=[tr, tok, tr],
        out_specs=tok,
        out_shape=jax.ShapeDtypeStruct((b, s, a), BF16),
        compiler_params=_params(("parallel", "parallel")),
        name="moba",
    )(qt, k, vt)


def _merge_kernel(yg_ref, at_ref, ga_ref, gb_ref, x_ref, wglu_ref, bglu_ref, wa_ref, wb_ref,
                  wo_ref, g_ref, o_ref):
    tm = x_ref.shape[1]
    yg = jnp.concatenate([yg_ref[o].reshape(tm, LANES) for o in range(yg_ref.shape[0])], axis=1)
    glu = jnp.dot(yg.astype(BF16), wglu_ref[...], preferred_element_type=F32) + bglu_ref[...]
    ya_in = (yg * jax.nn.sigmoid(glu)).astype(BF16)
    ya = jnp.dot(ya_in, wa_ref[...], preferred_element_type=F32)
    yb = jnp.dot(at_ref[0], wb_ref[...], preferred_element_type=F32)
    mixed = (ga_ref[0].astype(F32) * ya + gb_ref[0].astype(F32) * yb).astype(BF16)
    mo = jnp.dot(mixed, wo_ref[...], preferred_element_type=F32)
    o_ref[0] = x_ref[0] + _rms(mo) * g_ref[...]


def _merge(yg5, at, ga, gb, x, wglu, bglu, wa, wb, wo, g):
    b, s, d = x.shape
    tm = min(TOKEN_TILE, s)
    tok = lambda w: pl.BlockSpec((1, tm, w), lambda bi, ti: (bi, ti, 0))
    consts = [wglu, bglu, wa, wb, wo, g]
    return pl.pallas_call(
        _merge_kernel,
        grid=(b, s // tm),
        in_specs=[_chunk_tile_spec(yg5.shape[0], tm), tok(at.shape[2]), tok(d), tok(d), tok(d)]
        + [_const_spec(c.shape) for c in consts],
        out_specs=tok(d),
        out_shape=jax.ShapeDtypeStruct((b, s, d), F32),
        compiler_params=_params(("parallel", "parallel")),
        name="merge",
    )(yg5, at, ga, gb, x, *consts)


def _mlp_ple_kernel(x_ref, p_ref, gpre_ref, w1_ref, w2_ref, gpost_ref, wple_ref, wpg_ref, gple_ref,
                    o_ref):
    x = x_ref[...]
    hm = (_rms(x) * gpre_ref[...]).astype(BF16)
    dff = w1_ref.shape[1]
    f = jnp.zeros(x.shape, F32)
    for c in range(0, dff, FF_CHUNK):
        hid = jnp.dot(hm, w1_ref[:, c:c + FF_CHUNK], preferred_element_type=F32)
        hid = jnp.square(jnp.maximum(hid, 0.0)).astype(BF16)
        f = f + jnp.dot(hid, w2_ref[c:c + FF_CHUNK, :], preferred_element_type=F32)
    x = x + _rms(f) * gpost_ref[...]
    e = jnp.dot(p_ref[...].astype(BF16), wple_ref[...], preferred_element_type=F32)
    e = e * jax.nn.sigmoid(jnp.dot(x.astype(BF16), wpg_ref[...], preferred_element_type=F32))
    o_ref[...] = x + _rms(e) * gple_ref[...]


def _mlp_ple(x, p, gpre, w1, w2, gpost, wple, wpg, gple):
    n, d = x.shape
    tm = min(TOKEN_TILE, n)
    tok = lambda w: pl.BlockSpec((tm, w), lambda ti: (ti, 0))
    consts = [gpre, w1, w2, gpost, wple, wpg, gple]
    return pl.pallas_call(
        _mlp_ple_kernel,
        grid=(n // tm,),
        in_specs=[tok(d), tok(p.shape[1])] + [_const_spec(c.shape) for c in consts],
        out_specs=tok(d),
        out_shape=jax.ShapeDtypeStruct((n, d), F32),
        compiler_params=_params(("parallel",)),
        name="mlp_ple",
    )(x, p, *consts)


def kernel(x, p, g_pre_mix, w_in, ssm_lam_re, ssm_lam_im, ssm_log_dt, ssm_b_re, ssm_b_im, ssm_c_re,
           ssm_c_im, ssm_d, w_glu, b_glu, w_branch_a, w_branch_b, w_out, g_post_mix, g_pre_mlp,
           w_mlp1, w_mlp2, g_post_mlp, w_ple, w_ple_gate, g_ple):
    b, s, d = x.shape
    depth = w_in.shape[0]
    wdt = w_glu.shape[1]
    a = w_branch_b.shape[1]
    hd = a // ATTN_HEADS
    row = lambda v: v.reshape(1, -1)
    for i in range(depth):
        w = w_in[i].astype(BF16)
        o_q, o_k, o_v, o_g = wdt, wdt + a, wdt + 2 * a, wdt + 3 * a
        u5, qt, k, vt, ga, gb = _in_proj(
            x, row(g_pre_mix[i]), w[:, :o_q], w[:, o_q:o_k].T, w[:, o_k:o_v], w[:, o_v:o_g].T,
            w[:, o_g:], float(hd) ** -0.5 * LOG2E)

        prep = _ssm_prep(ssm_lam_re[i], ssm_lam_im[i], ssm_log_dt[i], ssm_b_re[i], ssm_b_im[i],
                         ssm_c_re[i], ssm_c_im[i])
        dvec = jnp.tile(ssm_d[i].reshape(-1, 1, LANES), (1, 1, SSM_CHUNK))
        yg5 = _s5(u5, *prep, dvec)

        at = _moba(qt, k, vt)

        x1 = _merge(yg5, at, ga, gb, x, w_glu[i].astype(BF16), row(b_glu[i]),
                    w_branch_a[i].astype(BF16), w_branch_b[i].astype(BF16), w_out[i].astype(BF16),
                    row(g_post_mix[i]))
        n = b * s
        x2 = _mlp_ple(x1.reshape(n, d), p[i].reshape(n, -1), row(g_pre_mlp[i]),
                      w_mlp1[i].astype(BF16), w_mlp2[i].astype(BF16), row(g_post_mlp[i]),
                      w_ple[i].astype(BF16), w_ple_gate[i].astype(BF16), row(g_ple[i]))
        x = x2.reshape(b, s, d)
    return x
```

```python
import functools

import jax
import jax.numpy as jnp
from jax import lax
from jax.experimental import pallas as pl
from jax.experimental.pallas import tpu as pltpu

F32 = jnp.float32
BF16 = jnp.bfloat16

SSM_GROUP = 16
SSM_STATE = 64
SSM_CHUNK = 8
LANES = 128
TILE_GROUPS = LANES // SSM_GROUP
SCAN_CHUNKS = 32
ATTN_HEADS = 8
MOBA_BLOCK = 256
MOBA_TOPK = 3
RMS_EPS = 1e-6
NEG_INF = -1e30
LOG2E = 1.4426950408889634
HEAD_PAIR = 2
SUM_ROWS = 16
SCORE_LOOKAHEAD = 2
VALUE_LAG = 2
TOKEN_TILE = 1024
ROW_SUB = 256
FF_CHUNK = 1024
VMEM_LIMIT = 48 * 1024 * 1024

_NT = (((1,), (1,)), ((), ()))


def _rms(v):
    return v * lax.rsqrt(jnp.mean(v * v, axis=-1, keepdims=True) + RMS_EPS)


def _const_spec(shape):
    nd = len(shape)
    return pl.BlockSpec(shape, lambda *_: (0,) * nd, pipeline_mode=pl.Buffered(1))


def _params(sem):
    return pltpu.CompilerParams(dimension_semantics=sem, vmem_limit_bytes=VMEM_LIMIT)


def _chunk_tile_spec(n_tiles, tm):
    return pl.BlockSpec((n_tiles, tm // SSM_CHUNK, None, SSM_CHUNK, LANES),
                        lambda bi, ti: (0, ti, bi, 0, 0))


def _in_proj_kernel(x_ref, g_ref, wu_ref, wqt_ref, wk_ref, wvt_ref, wg_ref,
                    u_ref, qt_ref, k_ref, vt_ref, ga_ref, gb_ref, *, q_scale):
    tm, d = x_ref.shape[1], x_ref.shape[2]
    sub = min(ROW_SUB, tm)
    tiles = [slice(r, r + sub) for r in range(0, tm, sub)]
    norm = lambda rs: (_rms(x_ref[0, rs]) * g_ref[...]).astype(BF16)
    hbs = [norm(tiles[0])]
    for n, rs in enumerate(tiles):
        hb = hbs[n]
        if n + 1 < len(tiles):
            hbs.append(norm(tiles[n + 1]))
        cs = slice(rs.start // SSM_CHUNK, rs.stop // SSM_CHUNK)
        u = jnp.dot(hb, wu_ref[...], preferred_element_type=F32)
        for o in range(u_ref.shape[0]):
            u_ref[o, cs] = u[:, o * LANES:(o + 1) * LANES].reshape(sub // SSM_CHUNK, SSM_CHUNK, LANES)
        k_ref[0, rs] = jnp.dot(hb, wk_ref[...], preferred_element_type=F32).astype(BF16)
        qt = lax.dot_general(wqt_ref[...], hb, _NT, preferred_element_type=F32)
        qt_ref[0, :, rs] = (qt * q_scale).astype(BF16)
        vt_ref[0, :, rs] = lax.dot_general(wvt_ref[...], hb, _NT,
                                           preferred_element_type=F32).astype(BF16)
        gates = jax.nn.sigmoid(jnp.dot(hb, wg_ref[...], preferred_element_type=F32))
        ga_ref[0, rs] = gates[:, :d].astype(BF16)
        gb_ref[0, rs] = gates[:, d:].astype(BF16)


def _in_proj(x, g, wu, wqt, wk, wvt, wg, q_scale):
    b, s, d = x.shape
    n_tiles = wu.shape[1] // LANES
    a = wk.shape[1]
    tm = min(TOKEN_TILE, s)
    tok = lambda w: pl.BlockSpec((1, tm, w), lambda bi, ti: (bi, ti, 0))
    tr = lambda w: pl.BlockSpec((1, w, tm), lambda bi, ti: (bi, 0, ti))
    return pl.pallas_call(
        functools.partial(_in_proj_kernel, q_scale=q_scale),
        grid=(b, s // tm),
        in_specs=[tok(d), _const_spec(g.shape), _const_spec(wu.shape), _const_spec(wqt.shape),
                  _const_spec(wk.shape), _const_spec(wvt.shape), _const_spec(wg.shape)],
        out_specs=[_chunk_tile_spec(n_tiles, tm), tr(a), tok(a), tr(a), tok(d), tok(d)],
        out_shape=[jax.ShapeDtypeStruct((n_tiles, s // SSM_CHUNK, b, SSM_CHUNK, LANES), F32),
                   jax.ShapeDtypeStruct((b, a, s), BF16), jax.ShapeDtypeStruct((b, s, a), BF16),
                   jax.ShapeDtypeStruct((b, a, s), BF16),
                   jax.ShapeDtypeStruct((b, s, d), BF16), jax.ShapeDtypeStruct((b, s, d), BF16)],
        compiler_params=_params(("parallel", "parallel")),
        name="in_proj",
    )(x, g, wu, wqt, wk, wvt, wg)


def _ssm_prep_kernel(lre_ref, lim_ref, ldt_ref, btr_ref, bti_ref, cr_ref, ci_ref,
                     toep_ref, wst_ref, wout_ref, ltr_ref, lti_ref):
    t = SSM_CHUNK
    rows, width = cr_ref.shape[1], cr_ref.shape[2]
    lre = lre_ref[0]
    lim = lim_ref[0]
    dt = jnp.exp(ldt_ref[0])
    ea = lre * dt
    eb = lim * dt

    def power(tau):
        mag = jnp.exp(tau * ea)
        return mag * jnp.cos(tau * eb), mag * jnp.sin(tau * eb)

    def times(pw, mr, mi):
        return pw[0] * mr - pw[1] * mi, pw[0] * mi + pw[1] * mr

    pw = [power(float(tau)) for tau in range(t + 1)]
    ltr_ref[0], lti_ref[0] = pw[t]
    same_group = (lax.broadcasted_iota(jnp.int32, (rows, width), 0) // SSM_GROUP
                  == lax.broadcasted_iota(jnp.int32, (rows, width), 1) // SSM_STATE)
    keep = lambda ref: jnp.where(same_group, ref[0], 0.0)
    cr, ci = keep(cr_ref), keep(ci_ref)
    nr = pw[1][0] - 1.0
    ni = pw[1][1]
    den = lre * lre + lim * lim
    coef = ((nr * lre + ni * lim) / den, (ni * lre - nr * lim) / den)
    bbr, bbi = times(coef, keep(btr_ref), keep(bti_ref))

    for s in range(t):
        wr, wi = times(pw[t - 1 - s], bbr, bbi)
        wst_ref[0, s * rows:(s + 1) * rows, :] = jnp.concatenate([wr, wi], axis=1).astype(BF16)
        orr, oi = times(pw[s + 1], cr, ci)
        wout_ref[0, :width, s * rows:(s + 1) * rows] = orr.T.astype(BF16)
        wout_ref[0, width:, s * rows:(s + 1) * rows] = (-oi).T.astype(BF16)

    cl = [times(pw[tau], cr, ci) for tau in range(t)]
    clr = jnp.concatenate([m[0] for m in cl], axis=0)
    cli = jnp.concatenate([m[1] for m in cl], axis=0)
    hi = lax.Precision.HIGHEST
    bd = (lax.dot_general(bbr, clr, _NT, precision=hi, preferred_element_type=F32)
          - lax.dot_general(bbi, cli, _NT, precision=hi, preferred_element_type=F32)).astype(BF16)
    for s in range(t):
        lead = [jnp.zeros((rows, s * rows), BF16)] if s else []
        toep_ref[0, s * rows:(s + 1) * rows, :] = jnp.concatenate(
            lead + [bd[:, :(t - s) * rows]], axis=1)


def _ssm_prep(lam_re, lam_im, log_dt, b_re, b_im, c_re, c_im):
    g, p = lam_re.shape
    tg, t = TILE_GROUPS, SSM_CHUNK
    n = g // tg
    width = tg * p
    lanes = lambda v: v.reshape(n, 1, width)
    chan = lambda m: jnp.tile(m.reshape(n, LANES, p), (1, 1, tg))
    vec = pl.BlockSpec((1, 1, width), lambda ti: (ti, 0, 0))
    mat = pl.BlockSpec((1, LANES, width), lambda ti: (ti, 0, 0))
    sq = lambda r, c: pl.BlockSpec((1, r, c), lambda ti: (ti, 0, 0))
    return pl.pallas_call(
        _ssm_prep_kernel,
        grid=(n,),
        in_specs=[vec, vec, vec, mat, mat, mat, mat],
        out_specs=[sq(t * LANES, t * LANES), sq(t * LANES, 2 * width), sq(2 * width, t * LANES),
                   vec, vec],
        out_shape=[jax.ShapeDtypeStruct((n, t * LANES, t * LANES), BF16),
                   jax.ShapeDtypeStruct((n, t * LANES, 2 * width), BF16),
                   jax.ShapeDtypeStruct((n, 2 * width, t * LANES), BF16),
                   jax.ShapeDtypeStruct((n, 1, width), F32), jax.ShapeDtypeStruct((n, 1, width), F32)],
        compiler_params=_params(("parallel",)),
        name="ssm_prep",
    )(lanes(lam_re), lanes(lam_im), lanes(jnp.repeat(log_dt, p)),
      chan(jnp.swapaxes(b_re, 1, 2)), chan(jnp.swapaxes(b_im, 1, 2)), chan(c_re), chan(c_im))


def _s5_kernel(u_ref, m_ref, wst_ref, wout_ref, lr_ref, li_ref, d_ref, y_ref, h_ref, st_ref, *, nb):
    t = SSM_CHUNK
    rows = u_ref.shape[0] // t
    half = lr_ref.shape[-1]

    @pl.when(pl.program_id(1) == 0)
    def _():
        st_ref[...] = jnp.zeros(st_ref.shape, F32)

    u32 = jnp.concatenate([u_ref[pl.ds(s, rows, stride=t), :] for s in range(t)], axis=1)
    u = u32.astype(BF16)
    h_ref[...] = jnp.dot(u, wst_ref[...], preferred_element_type=F32)
    lam_r = lr_ref[...]
    lam_i = li_ref[...]

    def step(k, carry):
        sr, si = carry
        rk = pl.ds(pl.multiple_of(k * nb, nb), nb)
        hr = h_ref[rk, :half]
        hi = h_ref[rk, half:]
        h_ref[rk, :half] = sr
        h_ref[rk, half:] = si
        return (lam_r * sr - lam_i * si + hr, lam_r * si + lam_i * sr + hi)

    y = jnp.dot(u, m_ref[...], preferred_element_type=F32) + d_ref[...] * u32
    sr, si = lax.fori_loop(0, rows // nb, step, (st_ref[0], st_ref[1]), unroll=True)
    st_ref[0] = sr
    st_ref[1] = si
    y = y + jnp.dot(h_ref[...].astype(BF16), wout_ref[...], preferred_element_type=F32)
    y = jax.nn.gelu(y)
    for s in range(t):
        y_ref[pl.ds(s, rows, stride=t), :] = y[:, s * LANES:(s + 1) * LANES]


def _s5(u5, toep, wst, wout, lam_r, lam_i, dvec):
    n_tiles, n_chunks, nb, t, lanes = u5.shape
    kb = min(SCAN_CHUNKS, n_chunks)
    blk_tokens = kb * nb * t
    flat = u5.reshape(n_tiles, n_chunks * nb * t, lanes)
    width = toep.shape[-1]
    st = wst.shape[-1]
    tok = pl.BlockSpec((None, blk_tokens, lanes), lambda oi, ki: (oi, ki, 0))
    op = lambda r, c: pl.BlockSpec((None, r, c), lambda oi, ki: (oi, 0, 0))
    y = pl.pallas_call(
        functools.partial(_s5_kernel, nb=nb),
        grid=(n_tiles, n_chunks // kb),
        in_specs=[tok, op(width, width), op(width, st), op(st, width), op(1, st // 2), op(1, st // 2),
                  op(1, width)],
        out_specs=tok,
        out_shape=jax.ShapeDtypeStruct(flat.shape, F32),
        scratch_shapes=[pltpu.VMEM((kb * nb, st), F32), pltpu.VMEM((2, nb, st // 2), F32)],
        compiler_params=_params(("parallel", "arbitrary")),
        name="s5",
    )(flat, toep, wst, wout, lam_r, lam_i, dvec)
    return y.reshape(u5.shape)


def _moba_kernel(qt_ref, k_ref, vt_ref, o_ref, *, n_blk, n_sel):
    blk = MOBA_BLOCK
    lanes = qt_ref.shape[1]
    hd = lanes // HEAD_PAIR
    gate_rows = 16
    means = [jnp.mean(k_ref[0, j * blk:(j + 1) * blk, :].astype(F32), axis=0, keepdims=True)
             for j in range(n_blk)]
    means.append(jnp.zeros((gate_rows - n_blk, lanes), F32))
    kmean = jnp.concatenate(means, axis=0).astype(BF16)
    head_row = lax.broadcasted_iota(jnp.int32, (lanes, blk), 0) // hd
    blk_row = lax.broadcasted_iota(jnp.int32, (gate_rows, blk), 0)
    causal = (lax.broadcasted_iota(jnp.int32, (blk, blk), 0)
              <= lax.broadcasted_iota(jnp.int32, (blk, blk), 1))
    ones = jnp.ones((SUM_ROWS, k_ref.shape[1]), BF16)
    v_ext = [jnp.concatenate([vt_ref[0, hh * hd:(hh + 1) * hd, :], ones], axis=0)
             for hh in range(HEAD_PAIR)]

    def scores(i, hh):
        qp = qt_ref[0, :, i * blk:(i + 1) * blk]
        qh = jnp.where(head_row == hh, qp, jnp.zeros_like(qp))
        s = jnp.dot(k_ref[0, 0:(i + 1) * blk, :], qh, preferred_element_type=F32)
        bias = [None] * i
        if i > n_sel:
            gate = jnp.dot(kmean, qh, preferred_element_type=F32)
            for n in range(i):
                gn = gate[n:n + 1, :]
                ahead = (gate > gn) | ((gate == gn) & (blk_row < n))
                ahead = ahead & (blk_row < i)
                rank = jnp.sum(ahead.astype(F32), axis=0, keepdims=True)
                bias[n] = jnp.where(rank < n_sel, 0.0, NEG_INF)
        return s, bias

    def attend(i, hh, s, bias):
        parts = []
        tops = []
        for j in range(i + 1):
            sj = s[j * blk:(j + 1) * blk, :]
            if j == i:
                sj = jnp.where(causal, sj, NEG_INF)
            top = jnp.max(sj, axis=0, keepdims=True)
            if j < i and bias[j] is not None:
                top = top + bias[j]
            parts.append(sj)
            tops.append(top)
        m = functools.reduce(jnp.maximum, tops)
        probs = []
        for j in range(i + 1):
            shift = m - bias[j] if (j < i and bias[j] is not None) else m
            probs.append(jnp.exp2(parts[j] - shift).astype(BF16))
        return jnp.concatenate(probs, axis=0) if i > 0 else probs[0]

    outs = []

    def values(i, hh, pt):
        o = jnp.dot(v_ext[hh][:, 0:(i + 1) * blk], pt, preferred_element_type=F32)
        outs.append(o[:hd] / o[hd:hd + 1])
        if hh == HEAD_PAIR - 1:
            ot = jnp.concatenate(outs, axis=0)
            o_ref[0, i * blk:(i + 1) * blk, :] = ot.T.astype(BF16)
            outs.clear()

    units = [(i, hh) for i in reversed(range(n_blk)) for hh in range(HEAD_PAIR)]
    pending = [scores(*u) for u in units[:SCORE_LOOKAHEAD]]
    probs_done = []
    for n, (i, hh) in enumerate(units):
        cur = pending.pop(0)
        if n + SCORE_LOOKAHEAD < len(units):
            pending.append(scores(*units[n + SCORE_LOOKAHEAD]))
        probs_done.append((i, hh, attend(i, hh, *cur)))
        if len(probs_done) > VALUE_LAG:
            values(*probs_done.pop(0))
    for item in probs_done:
        values(*item)


def _moba(qt, k, vt):
    b, a, s = qt.shape
    lanes = HEAD_PAIR * (a // ATTN_HEADS)
    n_blk = s // MOBA_BLOCK
    tr = pl.BlockSpec((1, lanes, s), lambda bi, hi: (bi, hi, 0))
    tok = pl.BlockSpec((1, s, lanes), lambda bi, hi: (bi, 0, hi))
    return pl.pallas_call(
        functools.partial(_moba_kernel, n_blk=n_blk, n_sel=min(MOBA_TOPK, n_blk - 1)),
        grid=(b, a // lanes),
        in_specs=[tr, tok, tr],
        out_specs=tok,
        out_shape=jax.ShapeDtypeStruct((b, s, a), BF16),
        compiler_params=_params(("parallel", "parallel")),
        name="moba",
    )(qt, k, vt)


def _merge_kernel(yg_ref, at_ref, ga_ref, gb_ref, x_ref, wglu_ref, bglu_ref, wa_ref, wb_ref,
                  wo_ref, g_ref, o_ref):
    tm = x_ref.shape[1]
    sub = min(ROW_SUB, tm)
    tiles = [slice(r, r + sub) for r in range(0, tm, sub)]
    chunks = [slice(rs.start // SSM_CHUNK, rs.stop // SSM_CHUNK) for rs in tiles]
    yg = [jnp.concatenate([yg_ref[o, cs].reshape(sub, LANES) for o in range(yg_ref.shape[0])], axis=1)
          for cs in chunks]
    glu = [jnp.dot(v.astype(BF16), wglu_ref[...], preferred_element_type=F32) + bglu_ref[...]
           for v in yg]
    ya_in = [(v * jax.nn.sigmoid(s)).astype(BF16) for v, s in zip(yg, glu)]
    ya = [jnp.dot(v, wa_ref[...], preferred_element_type=F32) for v in ya_in]
    yb = [jnp.dot(at_ref[0, rs], wb_ref[...], preferred_element_type=F32) for rs in tiles]
    mixed = [(ga_ref[0, rs].astype(F32) * a + gb_ref[0, rs].astype(F32) * b).astype(BF16)
             for rs, a, b in zip(tiles, ya, yb)]
    mo = [jnp.dot(v, wo_ref[...], preferred_element_type=F32) for v in mixed]
    for rs, v in zip(tiles, mo):
        o_ref[0, rs] = x_ref[0, rs] + _rms(v) * g_ref[...]


def _merge(yg5, at, ga, gb, x, wglu, bglu, wa, wb, wo, g):
    b, s, d = x.shape
    tm = min(TOKEN_TILE, s)
    tok = lambda w: pl.BlockSpec((1, tm, w), lambda bi, ti: (bi, ti, 0))
    consts = [wglu, bglu, wa, wb, wo, g]
    return pl.pallas_call(
        _merge_kernel,
        grid=(b, s // tm),
        in_specs=[_chunk_tile_spec(yg5.shape[0], tm), tok(at.shape[2]), tok(d), tok(d), tok(d)]
        + [_const_spec(c.shape) for c in consts],
        out_specs=tok(d),
        out_shape=jax.ShapeDtypeStruct((b, s, d), F32),
        compiler_params=_params(("parallel", "parallel")),
        name="merge",
    )(yg5, at, ga, gb, x, *consts)


def _mlp_ple_kernel(x_ref, p_ref, gpre_ref, w1_ref, w2_ref, gpost_ref, wple_ref, wpg_ref, gple_ref,
                    o_ref):
    tm = x_ref.shape[0]
    sub = min(ROW_SUB, tm)
    tiles = [slice(r, r + sub) for r in range(0, tm, sub)]
    dff = w1_ref.shape[1]
    xs = [x_ref[rs, :] for rs in tiles]
    hm = [(_rms(x) * gpre_ref[...]).astype(BF16) for x in xs]
    f = [None] * len(tiles)
    for c in range(0, dff, FF_CHUNK):
        hid = [jnp.dot(h, w1_ref[:, c:c + FF_CHUNK], preferred_element_type=F32) for h in hm]
        hid = [jnp.square(jnp.maximum(h, 0.0)).astype(BF16) for h in hid]
        part = [jnp.dot(h, w2_ref[c:c + FF_CHUNK, :], preferred_element_type=F32) for h in hid]
        f = [v if acc is None else acc + v for acc, v in zip(f, part)]
    xs = [x + _rms(v) * gpost_ref[...] for x, v in zip(xs, f)]
    e = [jnp.dot(p_ref[rs, :].astype(BF16), wple_ref[...], preferred_element_type=F32) for rs in tiles]
    gate = [jnp.dot(x.astype(BF16), wpg_ref[...], preferred_element_type=F32) for x in xs]
    for rs, x, ev, gv in zip(tiles, xs, e, gate):
        o_ref[rs, :] = x + _rms(ev * jax.nn.sigmoid(gv)) * gple_ref[...]


def _mlp_ple(x, p, gpre, w1, w2, gpost, wple, wpg, gple):
    n, d = x.shape
    tm = min(TOKEN_TILE, n)
    tok = lambda w: pl.BlockSpec((tm, w), lambda ti: (ti, 0))
    consts = [gpre, w1, w2, gpost, wple, wpg, gple]
    return pl.pallas_call(
        _mlp_ple_kernel,
        grid=(n // tm,),
        in_specs=[tok(d), tok(p.shape[1])] + [_const_spec(c.shape) for c in consts],
        out_specs=tok(d),
        out_shape=jax.ShapeDtypeStruct((n, d), F32),
        compiler_params=_params(("parallel",)),
        name="mlp_ple",
    )(x, p, *consts)


def kernel(x, p, g_pre_mix, w_in, ssm_lam_re, ssm_lam_im, ssm_log_dt, ssm_b_re, ssm_b_im, ssm_c_re,
           ssm_c_im, ssm_d, w_glu, b_glu, w_branch_a, w_branch_b, w_out, g_post_mix, g_pre_mlp,
           w_mlp1, w_mlp2, g_post_mlp, w_ple, w_ple_gate, g_ple):
    b, s, d = x.shape
    depth = w_in.shape[0]
    wdt = w_glu.shape[1]
    a = w_branch_b.shape[1]
    hd = a // ATTN_HEADS
    row = lambda v: v.reshape(1, -1)
    for i in range(depth):
        w = w_in[i].astype(BF16)
        o_q, o_k, o_v, o_g = wdt, wdt + a, wdt + 2 * a, wdt + 3 * a
        u5, qt, k, vt, ga, gb = _in_proj(
            x, row(g_pre_mix[i]), w[:, :o_q], w[:, o_q:o_k].T, w[:, o_k:o_v], w[:, o_v:o_g].T,
            w[:, o_g:], float(hd) ** -0.5 * LOG2E)

        prep = _ssm_prep(ssm_lam_re[i], ssm_lam_im[i], ssm_log_dt[i], ssm_b_re[i], ssm_b_im[i],
                         ssm_c_re[i], ssm_c_im[i])
        dvec = jnp.tile(ssm_d[i].reshape(-1, 1, LANES), (1, 1, SSM_CHUNK))
        yg5 = _s5(u5, *prep, dvec)

        at = _moba(qt, k, vt)

        x1 = _merge(yg5, at, ga, gb, x, w_glu[i].astype(BF16), row(b_glu[i]),
                    w_branch_a[i].astype(BF16), w_branch_b[i].astype(BF16), w_out[i].astype(BF16),
                    row(g_post_mix[i]))
        n = b * s
        x2 = _mlp_ple(x1.reshape(n, d), p[i].reshape(n, -1), row(g_pre_mlp[i]),
                      w_mlp1[i].astype(BF16), w_mlp2[i].astype(BF16), row(g_post_mlp[i]),
                      w_ple[i].astype(BF16), w_ple_gate[i].astype(BF16), row(g_ple[i]))
        x = x2.reshape(b, s, d)
    return x
```

```python
import functools

import jax
import jax.numpy as jnp
from jax import lax
from jax.experimental import pallas as pl
from jax.experimental.pallas import tpu as pltpu

F32 = jnp.float32
BF16 = jnp.bfloat16

SSM_GROUP = 16
SSM_STATE = 64
SSM_CHUNK = 8
LANES = 128
TILE_GROUPS = LANES // SSM_GROUP
SCAN_CHUNKS = 32
ATTN_HEADS = 8
MOBA_BLOCK = 256
MOBA_TOPK = 3
RMS_EPS = 1e-6
NEG_INF = -1e30
LOG2E = 1.4426950408889634
HEAD_PAIR = 2
MOBA_STEP_HEADS = 4
SUM_ROWS = 16
SCORE_LOOKAHEAD = 2
VALUE_LAG = 2
TOKEN_TILE = 1024
POST_TILE = 512
ROW_SUB = 256
FF_CHUNK = 1024
VMEM_LIMIT = 48 * 1024 * 1024

_NT = (((1,), (1,)), ((), ()))


def _rms(v):
    return v * lax.rsqrt(jnp.mean(v * v, axis=-1, keepdims=True) + RMS_EPS)


def _const_spec(shape):
    nd = len(shape)
    return pl.BlockSpec(shape, lambda *_: (0,) * nd, pipeline_mode=pl.Buffered(1))


def _params(sem):
    return pltpu.CompilerParams(dimension_semantics=sem, vmem_limit_bytes=VMEM_LIMIT)


def _chunk_tile_spec(n_tiles, tm):
    return pl.BlockSpec((n_tiles, tm // SSM_CHUNK, None, SSM_CHUNK, LANES),
                        lambda bi, ti: (0, ti, bi, 0, 0))


def _in_proj_kernel(x_ref, g_ref, wu_ref, wqt_ref, wk_ref, wvt_ref, wg_ref,
                    u_ref, qt_ref, k_ref, vt_ref, ga_ref, gb_ref, *, q_scale):
    tm, d = x_ref.shape[1], x_ref.shape[2]
    sub = min(ROW_SUB, tm)
    tiles = [slice(r, r + sub) for r in range(0, tm, sub)]
    norm = lambda rs: (_rms(x_ref[0, rs]) * g_ref[...]).astype(BF16)
    hbs = [norm(tiles[0])]
    for n, rs in enumerate(tiles):
        hb = hbs[n]
        if n + 1 < len(tiles):
            hbs.append(norm(tiles[n + 1]))
        cs = slice(rs.start // SSM_CHUNK, rs.stop // SSM_CHUNK)
        u = jnp.dot(hb, wu_ref[...], preferred_element_type=F32)
        for o in range(u_ref.shape[0]):
            u_ref[o, cs] = u[:, o * LANES:(o + 1) * LANES].reshape(sub // SSM_CHUNK, SSM_CHUNK, LANES)
        k_ref[0, rs] = jnp.dot(hb, wk_ref[...], preferred_element_type=F32).astype(BF16)
        qt = lax.dot_general(wqt_ref[...], hb, _NT, preferred_element_type=F32)
        qt_ref[0, :, rs] = (qt * q_scale).astype(BF16)
        vt_ref[0, :, rs] = lax.dot_general(wvt_ref[...], hb, _NT,
                                           preferred_element_type=F32).astype(BF16)
        gates = jax.nn.sigmoid(jnp.dot(hb, wg_ref[...], preferred_element_type=F32))
        ga_ref[0, rs] = gates[:, :d].astype(BF16)
        gb_ref[0, rs] = gates[:, d:].astype(BF16)


def _in_proj(x, g, wu, wqt, wk, wvt, wg, q_scale):
    b, s, d = x.shape
    n_tiles = wu.shape[1] // LANES
    a = wk.shape[1]
    tm = min(TOKEN_TILE, s)
    tok = lambda w: pl.BlockSpec((1, tm, w), lambda bi, ti: (bi, ti, 0))
    tr = lambda w: pl.BlockSpec((1, w, tm), lambda bi, ti: (bi, 0, ti))
    return pl.pallas_call(
        functools.partial(_in_proj_kernel, q_scale=q_scale),
        grid=(b, s // tm),
        in_specs=[tok(d), _const_spec(g.shape), _const_spec(wu.shape), _const_spec(wqt.shape),
                  _const_spec(wk.shape), _const_spec(wvt.shape), _const_spec(wg.shape)],
        out_specs=[_chunk_tile_spec(n_tiles, tm), tr(a), tok(a), tr(a), tok(d), tok(d)],
        out_shape=[jax.ShapeDtypeStruct((n_tiles, s // SSM_CHUNK, b, SSM_CHUNK, LANES), F32),
                   jax.ShapeDtypeStruct((b, a, s), BF16), jax.ShapeDtypeStruct((b, s, a), BF16),
                   jax.ShapeDtypeStruct((b, a, s), BF16),
                   jax.ShapeDtypeStruct((b, s, d), BF16), jax.ShapeDtypeStruct((b, s, d), BF16)],
        compiler_params=_params(("parallel", "parallel")),
        name="in_proj",
    )(x, g, wu, wqt, wk, wvt, wg)


def _ssm_prep_kernel(lre_ref, lim_ref, ldt_ref, btr_ref, bti_ref, cr_ref, ci_ref,
                     toep_ref, wst_ref, wout_ref, ltr_ref, lti_ref):
    t = SSM_CHUNK
    rows, width = cr_ref.shape[1], cr_ref.shape[2]
    lre = lre_ref[0]
    lim = lim_ref[0]
    dt = jnp.exp(ldt_ref[0])
    ea = lre * dt
    eb = lim * dt

    def power(tau):
        mag = jnp.exp(tau * ea)
        return mag * jnp.cos(tau * eb), mag * jnp.sin(tau * eb)

    def times(pw, mr, mi):
        return pw[0] * mr - pw[1] * mi, pw[0] * mi + pw[1] * mr

    pw = [power(float(tau)) for tau in range(t + 1)]
    ltr_ref[0], lti_ref[0] = pw[t]
    same_group = (lax.broadcasted_iota(jnp.int32, (rows, width), 0) // SSM_GROUP
                  == lax.broadcasted_iota(jnp.int32, (rows, width), 1) // SSM_STATE)
    keep = lambda ref: jnp.where(same_group, ref[0], 0.0)
    cr, ci = keep(cr_ref), keep(ci_ref)
    nr = pw[1][0] - 1.0
    ni = pw[1][1]
    den = lre * lre + lim * lim
    coef = ((nr * lre + ni * lim) / den, (ni * lre - nr * lim) / den)
    bbr, bbi = times(coef, keep(btr_ref), keep(bti_ref))

    for s in range(t):
        wr, wi = times(pw[t - 1 - s], bbr, bbi)
        wst_ref[0, s * rows:(s + 1) * rows, :] = jnp.concatenate([wr, wi], axis=1).astype(BF16)
        orr, oi = times(pw[s + 1], cr, ci)
        wout_ref[0, :width, s * rows:(s + 1) * rows] = orr.T.astype(BF16)
        wout_ref[0, width:, s * rows:(s + 1) * rows] = (-oi).T.astype(BF16)

    cl = [times(pw[tau], cr, ci) for tau in range(t)]
    clr = jnp.concatenate([m[0] for m in cl], axis=0)
    cli = jnp.concatenate([m[1] for m in cl], axis=0)
    hi = lax.Precision.HIGHEST
    bd = (lax.dot_general(bbr, clr, _NT, precision=hi, preferred_element_type=F32)
          - lax.dot_general(bbi, cli, _NT, precision=hi, preferred_element_type=F32)).astype(BF16)
    for s in range(t):
        lead = [jnp.zeros((rows, s * rows), BF16)] if s else []
        toep_ref[0, s * rows:(s + 1) * rows, :] = jnp.concatenate(
            lead + [bd[:, :(t - s) * rows]], axis=1)


def _ssm_prep(lam_re, lam_im, log_dt, b_re, b_im, c_re, c_im):
    g, p = lam_re.shape
    tg, t = TILE_GROUPS, SSM_CHUNK
    n = g // tg
    width = tg * p
    lanes = lambda v: v.reshape(n, 1, width)
    chan = lambda m: jnp.tile(m.reshape(n, LANES, p), (1, 1, tg))
    vec = pl.BlockSpec((1, 1, width), lambda ti: (ti, 0, 0))
    mat = pl.BlockSpec((1, LANES, width), lambda ti: (ti, 0, 0))
    sq = lambda r, c: pl.BlockSpec((1, r, c), lambda ti: (ti, 0, 0))
    return pl.pallas_call(
        _ssm_prep_kernel,
        grid=(n,),
        in_specs=[vec, vec, vec, mat, mat, mat, mat],
        out_specs=[sq(t * LANES, t * LANES), sq(t * LANES, 2 * width), sq(2 * width, t * LANES),
                   vec, vec],
        out_shape=[jax.ShapeDtypeStruct((n, t * LANES, t * LANES), BF16),
                   jax.ShapeDtypeStruct((n, t * LANES, 2 * width), BF16),
                   jax.ShapeDtypeStruct((n, 2 * width, t * LANES), BF16),
                   jax.ShapeDtypeStruct((n, 1, width), F32), jax.ShapeDtypeStruct((n, 1, width), F32)],
        compiler_params=_params(("parallel",)),
        name="ssm_prep",
    )(lanes(lam_re), lanes(lam_im), lanes(jnp.repeat(log_dt, p)),
      chan(jnp.swapaxes(b_re, 1, 2)), chan(jnp.swapaxes(b_im, 1, 2)), chan(c_re), chan(c_im))


def _s5_kernel(u_ref, m_ref, wst_ref, wout_ref, lr_ref, li_ref, d_ref, y_ref, h_ref, st_ref, *, nb):
    t = SSM_CHUNK
    rows = u_ref.shape[0] // t
    half = lr_ref.shape[-1]

    @pl.when(pl.program_id(1) == 0)
    def _():
        st_ref[...] = jnp.zeros(st_ref.shape, F32)

    u32 = jnp.concatenate([u_ref[pl.ds(s, rows, stride=t), :] for s in range(t)], axis=1)
    u = u32.astype(BF16)
    h_ref[...] = jnp.dot(u, wst_ref[...], preferred_element_type=F32)
    lam_r = lr_ref[...]
    lam_i = li_ref[...]

    def step(k, carry):
        sr, si = carry
        rk = pl.ds(pl.multiple_of(k * nb, nb), nb)
        hr = h_ref[rk, :half]
        hi = h_ref[rk, half:]
        h_ref[rk, :half] = sr
        h_ref[rk, half:] = si
        return (lam_r * sr - lam_i * si + hr, lam_r * si + lam_i * sr + hi)

    y = jnp.dot(u, m_ref[...], preferred_element_type=F32) + d_ref[...] * u32
    sr, si = lax.fori_loop(0, rows // nb, step, (st_ref[0], st_ref[1]), unroll=True)
    st_ref[0] = sr
    st_ref[1] = si
    y = y + jnp.dot(h_ref[...].astype(BF16), wout_ref[...], preferred_element_type=F32)
    y = jax.nn.gelu(y)
    for s in range(t):
        y_ref[pl.ds(s, rows, stride=t), :] = y[:, s * LANES:(s + 1) * LANES]


def _s5(u5, toep, wst, wout, lam_r, lam_i, dvec):
    n_tiles, n_chunks, nb, t, lanes = u5.shape
    kb = min(SCAN_CHUNKS, n_chunks)
    blk_tokens = kb * nb * t
    flat = u5.reshape(n_tiles, n_chunks * nb * t, lanes)
    width = toep.shape[-1]
    st = wst.shape[-1]
    tok = pl.BlockSpec((None, blk_tokens, lanes), lambda oi, ki: (oi, ki, 0))
    op = lambda r, c: pl.BlockSpec((None, r, c), lambda oi, ki: (oi, 0, 0))
    y = pl.pallas_call(
        functools.partial(_s5_kernel, nb=nb),
        grid=(n_tiles, n_chunks // kb),
        in_specs=[tok, op(width, width), op(width, st), op(st, width), op(1, st // 2), op(1, st // 2),
                  op(1, width)],
        out_specs=tok,
        out_shape=jax.ShapeDtypeStruct(flat.shape, F32),
        scratch_shapes=[pltpu.VMEM((kb * nb, st), F32), pltpu.VMEM((2, nb, st // 2), F32)],
        compiler_params=_params(("parallel", "arbitrary")),
        name="s5",
    )(flat, toep, wst, wout, lam_r, lam_i, dvec)
    return y.reshape(u5.shape)


def _moba_kernel(qt_ref, k_ref, vt_ref, o_ref, *, n_blk, n_sel):
    blk = MOBA_BLOCK
    lanes = LANES
    hd = lanes // HEAD_PAIR
    n_heads = qt_ref.shape[1] // hd
    pair_lanes = lambda h: slice((h // HEAD_PAIR) * lanes, (h // HEAD_PAIR + 1) * lanes)
    gate_rows = 16
    means = [jnp.mean(k_ref[0, j * blk:(j + 1) * blk, :].astype(F32), axis=0, keepdims=True)
             for j in range(n_blk)]
    means.append(jnp.zeros((gate_rows - n_blk, k_ref.shape[2]), F32))
    kmean = jnp.concatenate(means, axis=0).astype(BF16)
    head_row = lax.broadcasted_iota(jnp.int32, (lanes, blk), 0) // hd
    blk_row = lax.broadcasted_iota(jnp.int32, (gate_rows, blk), 0)
    causal = (lax.broadcasted_iota(jnp.int32, (blk, blk), 0)
              <= lax.broadcasted_iota(jnp.int32, (blk, blk), 1))
    ones = jnp.ones((SUM_ROWS, k_ref.shape[1]), BF16)
    v_ext = [jnp.concatenate([vt_ref[0, h * hd:(h + 1) * hd, :], ones], axis=0)
             for h in range(n_heads)]

    def scores(i, h):
        qp = qt_ref[0, pair_lanes(h), i * blk:(i + 1) * blk]
        qh = jnp.where(head_row == h % HEAD_PAIR, qp, jnp.zeros_like(qp))
        s = jnp.dot(k_ref[0, 0:(i + 1) * blk, pair_lanes(h)], qh, preferred_element_type=F32)
        bias = [None] * i
        if i > n_sel:
            gate = jnp.dot(kmean[:, pair_lanes(h)], qh, preferred_element_type=F32)
            for n in range(i):
                gn = gate[n:n + 1, :]
                ahead = (gate > gn) | ((gate == gn) & (blk_row < n))
                ahead = ahead & (blk_row < i)
                rank = jnp.sum(ahead.astype(F32), axis=0, keepdims=True)
                bias[n] = jnp.where(rank < n_sel, 0.0, NEG_INF)
        return s, bias

    def attend(i, s, bias):
        parts = []
        tops = []
        for j in range(i + 1):
            sj = s[j * blk:(j + 1) * blk, :]
            if j == i:
                sj = jnp.where(causal, sj, NEG_INF)
            top = jnp.max(sj, axis=0, keepdims=True)
            if j < i and bias[j] is not None:
                top = top + bias[j]
            parts.append(sj)
            tops.append(top)
        m = functools.reduce(jnp.maximum, tops)
        probs = []
        for j in range(i + 1):
            shift = m - bias[j] if (j < i and bias[j] is not None) else m
            probs.append(jnp.exp2(parts[j] - shift).astype(BF16))
        return jnp.concatenate(probs, axis=0) if i > 0 else probs[0]

    outs = []

    def values(i, h, pt):
        o = jnp.dot(v_ext[h][:, 0:(i + 1) * blk], pt, preferred_element_type=F32)
        outs.append(o[:hd] / o[hd:hd + 1])
        if h % HEAD_PAIR == HEAD_PAIR - 1:
            ot = jnp.concatenate(outs, axis=0)
            o_ref[0, i * blk:(i + 1) * blk, pair_lanes(h)] = ot.T.astype(BF16)
            outs.clear()

    units = [(i, h) for i in reversed(range(n_blk)) for h in range(n_heads)]
    pending = [scores(*u) for u in units[:SCORE_LOOKAHEAD]]
    probs_done = []
    for n, (i, h) in enumerate(units):
        cur = pending.pop(0)
        if n + SCORE_LOOKAHEAD < len(units):
            pending.append(scores(*units[n + SCORE_LOOKAHEAD]))
        probs_done.append((i, h, attend(i, *cur)))
        if len(probs_done) > VALUE_LAG:
            values(*probs_done.pop(0))
    for item in probs_done:
        values(*item)


def _moba(qt, k, vt):
    b, a, s = qt.shape
    lanes = MOBA_STEP_HEADS * (a // ATTN_HEADS)
    n_blk = s // MOBA_BLOCK
    tr = pl.BlockSpec((1, lanes, s), lambda bi, hi: (bi, hi, 0))
    tok = pl.BlockSpec((1, s, lanes), lambda bi, hi: (bi, 0, hi))
    return pl.pallas_call(
        functools.partial(_moba_kernel, n_blk=n_blk, n_sel=min(MOBA_TOPK, n_blk - 1)),
        grid=(b, a // lanes),
        in_specs=[tr, tok, tr],
        out_specs=tok,
        out_shape=jax.ShapeDtypeStruct((b, s, a), BF16),
        compiler_params=_params(("parallel", "parallel")),
        name="moba",
    )(qt, k, vt)


def _post_kernel(yg_ref, at_ref, ga_ref, gb_ref, x_ref, p_ref, wglu_ref, bglu_ref, wa_ref, wb_ref,
                 wo_ref, gmix_ref, gpre_ref, w1_ref, w2_ref, gpost_ref, wple_ref, wpg_ref, gple_ref,
                 o_ref):
    tm = x_ref.shape[1]
    sub = min(ROW_SUB, tm)
    dff = w1_ref.shape[1]
    mm = lambda a, w: jnp.dot(a, w, preferred_element_type=F32)
    tiles = [slice(r, r + sub) for r in range(0, tm, sub)]
    chunks = [slice(rs.start // SSM_CHUNK, rs.stop // SSM_CHUNK) for rs in tiles]
    yg = [jnp.concatenate([yg_ref[o, cs].reshape(sub, LANES) for o in range(yg_ref.shape[0])], axis=1)
          for cs in chunks]
    glu = [mm(v.astype(BF16), wglu_ref[...]) + bglu_ref[...] for v in yg]
    ya_in = [(v * jax.nn.sigmoid(s)).astype(BF16) for v, s in zip(yg, glu)]
    ya = [mm(v, wa_ref[...]) for v in ya_in]
    yb = [mm(at_ref[0, rs], wb_ref[...]) for rs in tiles]
    mixed = [(ga_ref[0, rs].astype(F32) * a + gb_ref[0, rs].astype(F32) * b).astype(BF16)
             for rs, a, b in zip(tiles, ya, yb)]
    mo = [mm(v, wo_ref[...]) for v in mixed]
    xs = [x_ref[0, rs] + _rms(v) * gmix_ref[...] for rs, v in zip(tiles, mo)]

    hm = [(_rms(x) * gpre_ref[...]).astype(BF16) for x in xs]
    f = [None] * len(tiles)
    for c in range(0, dff, FF_CHUNK):
        hid = [mm(h, w1_ref[:, c:c + FF_CHUNK]) for h in hm]
        hid = [jnp.square(jnp.maximum(h, 0.0)).astype(BF16) for h in hid]
        part = [mm(h, w2_ref[c:c + FF_CHUNK, :]) for h in hid]
        f = [v if acc is None else acc + v for acc, v in zip(f, part)]
    xs = [x + _rms(v) * gpost_ref[...] for x, v in zip(xs, f)]

    e = [mm(p_ref[0, rs].astype(BF16), wple_ref[...]) for rs in tiles]
    gate = [mm(x.astype(BF16), wpg_ref[...]) for x in xs]
    for rs, x, ev, gv in zip(tiles, xs, e, gate):
        o_ref[0, rs] = x + _rms(ev * jax.nn.sigmoid(gv)) * gple_ref[...]


def _post(yg5, at, ga, gb, x, p, consts):
    b, s, d = x.shape
    tm = min(POST_TILE, s)
    tok = lambda w: pl.BlockSpec((1, tm, w), lambda bi, ti: (bi, ti, 0))
    return pl.pallas_call(
        _post_kernel,
        grid=(b, s // tm),
        in_specs=[_chunk_tile_spec(yg5.shape[0], tm), tok(at.shape[2]), tok(d), tok(d), tok(d),
                  tok(p.shape[2])] + [_const_spec(c.shape) for c in consts],
        out_specs=tok(d),
        out_shape=jax.ShapeDtypeStruct((b, s, d), F32),
        compiler_params=_params(("parallel", "parallel")),
        name="post",
    )(yg5, at, ga, gb, x, p, *consts)


def kernel(x, p, g_pre_mix, w_in, ssm_lam_re, ssm_lam_im, ssm_log_dt, ssm_b_re, ssm_b_im, ssm_c_re,
           ssm_c_im, ssm_d, w_glu, b_glu, w_branch_a, w_branch_b, w_out, g_post_mix, g_pre_mlp,
           w_mlp1, w_mlp2, g_post_mlp, w_ple, w_ple_gate, g_ple):
    b, s, d = x.shape
    depth = w_in.shape[0]
    wdt = w_glu.shape[1]
    a = w_branch_b.shape[1]
    hd = a // ATTN_HEADS
    row = lambda v: v.reshape(1, -1)
    for i in range(depth):
        w = w_in[i].astype(BF16)
        o_q, o_k, o_v, o_g = wdt, wdt + a, wdt + 2 * a, wdt + 3 * a
        u5, qt, k, vt, ga, gb = _in_proj(
            x, row(g_pre_mix[i]), w[:, :o_q], w[:, o_q:o_k].T, w[:, o_k:o_v], w[:, o_v:o_g].T,
            w[:, o_g:], float(hd) ** -0.5 * LOG2E)

        prep = _ssm_prep(ssm_lam_re[i], ssm_lam_im[i], ssm_log_dt[i], ssm_b_re[i], ssm_b_im[i],
                         ssm_c_re[i], ssm_c_im[i])
        dvec = jnp.tile(ssm_d[i].reshape(-1, 1, LANES), (1, 1, SSM_CHUNK))
        yg5 = _s5(u5, *prep, dvec)

        at = _moba(qt, k, vt)

        x = _post(yg5, at, ga, gb, x, p[i],
                  [w_glu[i].astype(BF16), row(b_glu[i]), w_branch_a[i].astype(BF16),
                   w_branch_b[i].astype(BF16), w_out[i].astype(BF16), row(g_post_mix[i]),
                   row(g_pre_mlp[i]), w_mlp1[i].astype(BF16), w_mlp2[i].astype(BF16),
                   row(g_post_mlp[i]), w_ple[i].astype(BF16), w_ple_gate[i].astype(BF16),
                   row(g_ple[i])])
    return x
```

```python
import functools

import jax
import jax.numpy as jnp
from jax import lax
from jax.experimental import pallas as pl
from jax.experimental.pallas import tpu as pltpu

F32 = jnp.float32
BF16 = jnp.bfloat16

SSM_GROUP = 16
SSM_STATE = 64
SSM_CHUNK = 8
LANES = 128
TILE_GROUPS = LANES // SSM_GROUP
S5_SUB_BLOCKS = 4
SCAN_CHUNKS = 32
ATTN_HEADS = 8
MOBA_BLOCK = 256
MOBA_TOPK = 3
RMS_EPS = 1e-6
NEG_INF = -1e30
LOG2E = 1.4426950408889634
HEAD_PAIR = 2
MOBA_STEP_HEADS = 8
SUM_ROWS = 16
SCORE_LOOKAHEAD = 2
VALUE_LAG = 2
TOKEN_TILE = 1024
POST_TILE = 512
ROW_SUB = 256
FF_CHUNK = 1024
VMEM_LIMIT = 48 * 1024 * 1024

_NT = (((1,), (1,)), ((), ()))


def _rms(v):
    return v * lax.rsqrt(jnp.mean(v * v, axis=-1, keepdims=True) + RMS_EPS)


def _const_spec(shape):
    nd = len(shape)
    return pl.BlockSpec(shape, lambda *_: (0,) * nd, pipeline_mode=pl.Buffered(1))


def _params(sem):
    return pltpu.CompilerParams(dimension_semantics=sem, vmem_limit_bytes=VMEM_LIMIT)


def _chunk_tile_spec(n_tiles, tm):
    return pl.BlockSpec((n_tiles, tm // SSM_CHUNK, None, SSM_CHUNK, LANES),
                        lambda bi, ti: (0, ti, bi, 0, 0))


def _in_proj_kernel(x_ref, g_ref, wu_ref, wqt_ref, wk_ref, wvt_ref, wg_ref,
                    u_ref, qt_ref, k_ref, vt_ref, ga_ref, gb_ref, *, q_scale):
    tm, d = x_ref.shape[1], x_ref.shape[2]
    sub = min(ROW_SUB, tm)
    tiles = [slice(r, r + sub) for r in range(0, tm, sub)]
    norm = lambda rs: (_rms(x_ref[0, rs]) * g_ref[...]).astype(BF16)
    hbs = [norm(tiles[0])]
    for n, rs in enumerate(tiles):
        hb = hbs[n]
        if n + 1 < len(tiles):
            hbs.append(norm(tiles[n + 1]))
        cs = slice(rs.start // SSM_CHUNK, rs.stop // SSM_CHUNK)
        u = jnp.dot(hb, wu_ref[...], preferred_element_type=F32)
        for o in range(u_ref.shape[0]):
            u_ref[o, cs] = u[:, o * LANES:(o + 1) * LANES].reshape(sub // SSM_CHUNK, SSM_CHUNK, LANES)
        k_ref[0, rs] = jnp.dot(hb, wk_ref[...], preferred_element_type=F32).astype(BF16)
        qt = lax.dot_general(wqt_ref[...], hb, _NT, preferred_element_type=F32)
        qt_ref[0, :, rs] = (qt * q_scale).astype(BF16)
        vt_ref[0, :, rs] = lax.dot_general(wvt_ref[...], hb, _NT,
                                           preferred_element_type=F32).astype(BF16)
        gates = jax.nn.sigmoid(jnp.dot(hb, wg_ref[...], preferred_element_type=F32))
        ga_ref[0, rs] = gates[:, :d].astype(BF16)
        gb_ref[0, rs] = gates[:, d:].astype(BF16)


def _in_proj(x, g, wu, wqt, wk, wvt, wg, q_scale):
    b, s, d = x.shape
    n_tiles = wu.shape[1] // LANES
    a = wk.shape[1]
    tm = min(TOKEN_TILE, s)
    tok = lambda w: pl.BlockSpec((1, tm, w), lambda bi, ti: (bi, ti, 0))
    tr = lambda w: pl.BlockSpec((1, w, tm), lambda bi, ti: (bi, 0, ti))
    return pl.pallas_call(
        functools.partial(_in_proj_kernel, q_scale=q_scale),
        grid=(b, s // tm),
        in_specs=[tok(d), _const_spec(g.shape), _const_spec(wu.shape), _const_spec(wqt.shape),
                  _const_spec(wk.shape), _const_spec(wvt.shape), _const_spec(wg.shape)],
        out_specs=[_chunk_tile_spec(n_tiles, tm), tr(a), tok(a), tr(a), tok(d), tok(d)],
        out_shape=[jax.ShapeDtypeStruct((n_tiles, s // SSM_CHUNK, b, SSM_CHUNK, LANES), F32),
                   jax.ShapeDtypeStruct((b, a, s), BF16), jax.ShapeDtypeStruct((b, s, a), BF16),
                   jax.ShapeDtypeStruct((b, a, s), BF16),
                   jax.ShapeDtypeStruct((b, s, d), BF16), jax.ShapeDtypeStruct((b, s, d), BF16)],
        compiler_params=_params(("parallel", "parallel")),
        name="in_proj",
    )(x, g, wu, wqt, wk, wvt, wg)


def _ssm_prep_kernel(lre_ref, lim_ref, ldt_ref, btr_ref, bti_ref, cr_ref, ci_ref,
                     toep_ref, wst_ref, wout_ref, ltr_ref, lti_ref):
    t = SSM_CHUNK
    rows, width = cr_ref.shape[1], cr_ref.shape[2]
    lre = lre_ref[0]
    lim = lim_ref[0]
    dt = jnp.exp(ldt_ref[0])
    ea = lre * dt
    eb = lim * dt

    def power(tau):
        mag = jnp.exp(tau * ea)
        return mag * jnp.cos(tau * eb), mag * jnp.sin(tau * eb)

    def times(pw, mr, mi):
        return pw[0] * mr - pw[1] * mi, pw[0] * mi + pw[1] * mr

    pw = [power(float(tau)) for tau in range(t + 1)]
    ltr_ref[0], lti_ref[0] = pw[t]
    same_group = (lax.broadcasted_iota(jnp.int32, (rows, width), 0) // SSM_GROUP
                  == lax.broadcasted_iota(jnp.int32, (rows, width), 1) // SSM_STATE)
    keep = lambda ref: jnp.where(same_group, ref[0], 0.0)
    cr, ci = keep(cr_ref), keep(ci_ref)
    nr = pw[1][0] - 1.0
    ni = pw[1][1]
    den = lre * lre + lim * lim
    coef = ((nr * lre + ni * lim) / den, (ni * lre - nr * lim) / den)
    bbr, bbi = times(coef, keep(btr_ref), keep(bti_ref))

    for s in range(t):
        wr, wi = times(pw[t - 1 - s], bbr, bbi)
        wst_ref[0, s * rows:(s + 1) * rows, :] = jnp.concatenate([wr, wi], axis=1).astype(BF16)
        orr, oi = times(pw[s + 1], cr, ci)
        wout_ref[0, :width, s * rows:(s + 1) * rows] = orr.T.astype(BF16)
        wout_ref[0, width:, s * rows:(s + 1) * rows] = (-oi).T.astype(BF16)

    cl = [times(pw[tau], cr, ci) for tau in range(t)]
    clr = jnp.concatenate([m[0] for m in cl], axis=0)
    cli = jnp.concatenate([m[1] for m in cl], axis=0)
    hi = lax.Precision.HIGHEST
    bd = (lax.dot_general(bbr, clr, _NT, precision=hi, preferred_element_type=F32)
          - lax.dot_general(bbi, cli, _NT, precision=hi, preferred_element_type=F32)).astype(BF16)
    for s in range(t):
        lead = [jnp.zeros((rows, s * rows), BF16)] if s else []
        toep_ref[0, s * rows:(s + 1) * rows, :] = jnp.concatenate(
            lead + [bd[:, :(t - s) * rows]], axis=1)


def _ssm_prep(lam_re, lam_im, log_dt, b_re, b_im, c_re, c_im):
    g, p = lam_re.shape
    tg, t = TILE_GROUPS, SSM_CHUNK
    n = g // tg
    width = tg * p
    lanes = lambda v: v.reshape(n, 1, width)
    chan = lambda m: jnp.tile(m.reshape(n, LANES, p), (1, 1, tg))
    vec = pl.BlockSpec((1, 1, width), lambda ti: (ti, 0, 0))
    mat = pl.BlockSpec((1, LANES, width), lambda ti: (ti, 0, 0))
    sq = lambda r, c: pl.BlockSpec((1, r, c), lambda ti: (ti, 0, 0))
    return pl.pallas_call(
        _ssm_prep_kernel,
        grid=(n,),
        in_specs=[vec, vec, vec, mat, mat, mat, mat],
        out_specs=[sq(t * LANES, t * LANES), sq(t * LANES, 2 * width), sq(2 * width, t * LANES),
                   vec, vec],
        out_shape=[jax.ShapeDtypeStruct((n, t * LANES, t * LANES), BF16),
                   jax.ShapeDtypeStruct((n, t * LANES, 2 * width), BF16),
                   jax.ShapeDtypeStruct((n, 2 * width, t * LANES), BF16),
                   jax.ShapeDtypeStruct((n, 1, width), F32), jax.ShapeDtypeStruct((n, 1, width), F32)],
        compiler_params=_params(("parallel",)),
        name="ssm_prep",
    )(lanes(lam_re), lanes(lam_im), lanes(jnp.repeat(log_dt, p)),
      chan(jnp.swapaxes(b_re, 1, 2)), chan(jnp.swapaxes(b_im, 1, 2)), chan(c_re), chan(c_im))


def _s5_kernel(u_ref, m_ref, wst_ref, wout_ref, lr_ref, li_ref, d_ref, y_ref, h_ref, st_ref, *, nb):
    t = SSM_CHUNK
    rows = u_ref.shape[0] // t
    half = lr_ref.shape[-1]

    @pl.when(pl.program_id(1) == 0)
    def _():
        st_ref[...] = jnp.zeros(st_ref.shape, F32)

    rp = rows // S5_SUB_BLOCKS
    parts = [slice(p * rp, (p + 1) * rp) for p in range(S5_SUB_BLOCKS)]
    u32 = [jnp.concatenate([u_ref[pl.ds(rs.start * t + s, rp, stride=t), :] for s in range(t)], axis=1)
           for rs in parts]
    u = [v.astype(BF16) for v in u32]
    for rs, v in zip(parts, u):
        h_ref[rs, :] = jnp.dot(v, wst_ref[...], preferred_element_type=F32)
    lam_r = lr_ref[...]
    lam_i = li_ref[...]

    def step(k, carry):
        sr, si = carry
        rk = pl.ds(pl.multiple_of(k * nb, nb), nb)
        hr = h_ref[rk, :half]
        hi = h_ref[rk, half:]
        h_ref[rk, :half] = sr
        h_ref[rk, half:] = si
        return (lam_r * sr - lam_i * si + hr, lam_r * si + lam_i * sr + hi)

    y = [jnp.dot(v, m_ref[...], preferred_element_type=F32) + d_ref[...] * v32
         for v, v32 in zip(u, u32)]
    sr, si = lax.fori_loop(0, rows // nb, step, (st_ref[0], st_ref[1]), unroll=True)
    st_ref[0] = sr
    st_ref[1] = si
    y = [v + jnp.dot(h_ref[rs, :].astype(BF16), wout_ref[...], preferred_element_type=F32)
         for rs, v in zip(parts, y)]
    for rs, v in zip(parts, y):
        v = jax.nn.gelu(v)
        for s in range(t):
            y_ref[pl.ds(rs.start * t + s, rp, stride=t), :] = v[:, s * LANES:(s + 1) * LANES]


def _s5(u5, toep, wst, wout, lam_r, lam_i, dvec):
    n_tiles, n_chunks, nb, t, lanes = u5.shape
    kb = min(SCAN_CHUNKS, n_chunks)
    blk_tokens = kb * nb * t
    flat = u5.reshape(n_tiles, n_chunks * nb * t, lanes)
    width = toep.shape[-1]
    st = wst.shape[-1]
    tok = pl.BlockSpec((None, blk_tokens, lanes), lambda oi, ki: (oi, ki, 0))
    op = lambda r, c: pl.BlockSpec((None, r, c), lambda oi, ki: (oi, 0, 0))
    y = pl.pallas_call(
        functools.partial(_s5_kernel, nb=nb),
        grid=(n_tiles, n_chunks // kb),
        in_specs=[tok, op(width, width), op(width, st), op(st, width), op(1, st // 2), op(1, st // 2),
                  op(1, width)],
        out_specs=tok,
        out_shape=jax.ShapeDtypeStruct(flat.shape, F32),
        scratch_shapes=[pltpu.VMEM((kb * nb, st), F32), pltpu.VMEM((2, nb, st // 2), F32)],
        compiler_params=_params(("parallel", "arbitrary")),
        name="s5",
    )(flat, toep, wst, wout, lam_r, lam_i, dvec)
    return y.reshape(u5.shape)


def _moba_kernel(qt_ref, k_ref, vt_ref, o_ref, *, n_blk, n_sel):
    blk = MOBA_BLOCK
    lanes = LANES
    hd = lanes // HEAD_PAIR
    n_heads = qt_ref.shape[1] // hd
    pair_lanes = lambda h: slice((h // HEAD_PAIR) * lanes, (h // HEAD_PAIR + 1) * lanes)
    gate_rows = 16
    means = [jnp.mean(k_ref[0, j * blk:(j + 1) * blk, :].astype(F32), axis=0, keepdims=True)
             for j in range(n_blk)]
    means.append(jnp.zeros((gate_rows - n_blk, k_ref.shape[2]), F32))
    kmean = jnp.concatenate(means, axis=0).astype(BF16)
    head_row = lax.broadcasted_iota(jnp.int32, (lanes, blk), 0) // hd
    blk_row = lax.broadcasted_iota(jnp.int32, (gate_rows, blk), 0)
    causal = (lax.broadcasted_iota(jnp.int32, (blk, blk), 0)
              <= lax.broadcasted_iota(jnp.int32, (blk, blk), 1))
    ones = jnp.ones((SUM_ROWS, k_ref.shape[1]), BF16)
    v_ext = [jnp.concatenate([vt_ref[0, h * hd:(h + 1) * hd, :], ones], axis=0)
             for h in range(n_heads)]

    def scores(i, h):
        qp = qt_ref[0, pair_lanes(h), i * blk:(i + 1) * blk]
        qh = jnp.where(head_row == h % HEAD_PAIR, qp, jnp.zeros_like(qp))
        s = jnp.dot(k_ref[0, 0:(i + 1) * blk, pair_lanes(h)], qh, preferred_element_type=F32)
        bias = [None] * i
        if i > n_sel:
            gate = jnp.dot(kmean[:, pair_lanes(h)], qh, preferred_element_type=F32)
            for n in range(i):
                gn = gate[n:n + 1, :]
                ahead = (gate > gn) | ((gate == gn) & (blk_row < n))
                ahead = ahead & (blk_row < i)
                rank = jnp.sum(ahead.astype(F32), axis=0, keepdims=True)
                bias[n] = jnp.where(rank < n_sel, 0.0, NEG_INF)
        return s, bias

    def attend(i, s, bias):
        parts = []
        tops = []
        for j in range(i + 1):
            sj = s[j * blk:(j + 1) * blk, :]
            if j == i:
                sj = jnp.where(causal, sj, NEG_INF)
            top = jnp.max(sj, axis=0, keepdims=True)
            if j < i and bias[j] is not None:
                top = top + bias[j]
            parts.append(sj)
            tops.append(top)
        m = functools.reduce(jnp.maximum, tops)
        probs = []
        for j in range(i + 1):
            shift = m - bias[j] if (j < i and bias[j] is not None) else m
            probs.append(jnp.exp2(parts[j] - shift).astype(BF16))
        return jnp.concatenate(probs, axis=0) if i > 0 else probs[0]

    outs = []

    def values(i, h, pt):
        o = jnp.dot(v_ext[h][:, 0:(i + 1) * blk], pt, preferred_element_type=F32)
        outs.append(o[:hd] / o[hd:hd + 1])
        if h % HEAD_PAIR == HEAD_PAIR - 1:
            ot = jnp.concatenate(outs, axis=0)
            o_ref[0, i * blk:(i + 1) * blk, pair_lanes(h)] = ot.T.astype(BF16)
            outs.clear()

    units = [(i, h) for i in reversed(range(n_blk)) for h in range(n_heads)]
    pending = [scores(*u) for u in units[:SCORE_LOOKAHEAD]]
    probs_done = []
    for n, (i, h) in enumerate(units):
        cur = pending.pop(0)
        if n + SCORE_LOOKAHEAD < len(units):
            pending.append(scores(*units[n + SCORE_LOOKAHEAD]))
        probs_done.append((i, h, attend(i, *cur)))
        if len(probs_done) > VALUE_LAG:
            values(*probs_done.pop(0))
    for item in probs_done:
        values(*item)


def _moba(qt, k, vt):
    b, a, s = qt.shape
    lanes = MOBA_STEP_HEADS * (a // ATTN_HEADS)
    n_blk = s // MOBA_BLOCK
    tr = pl.BlockSpec((1, lanes, s), lambda bi, hi: (bi, hi, 0))
    tok = pl.BlockSpec((1, s, lanes), lambda bi, hi: (bi, 0, hi))
    return pl.pallas_call(
        functools.partial(_moba_kernel, n_blk=n_blk, n_sel=min(MOBA_TOPK, n_blk - 1)),
        grid=(b, a // lanes),
        in_specs=[tr, tok, tr],
        out_specs=tok,
        out_shape=jax.ShapeDtypeStruct((b, s, a), BF16),
        compiler_params=_params(("parallel", "parallel")),
        name="moba",
    )(qt, k, vt)


def _post_kernel(yg_ref, at_ref, ga_ref, gb_ref, x_ref, p_ref, wglu_ref, bglu_ref, wa_ref, wb_ref,
                 wo_ref, gmix_ref, gpre_ref, w1_ref, w2_ref, gpost_ref, wple_ref, wpg_ref, gple_ref,
                 o_ref):
    tm = x_ref.shape[1]
    sub = min(ROW_SUB, tm)
    dff = w1_ref.shape[1]
    mm = lambda a, w: jnp.dot(a, w, preferred_element_type=F32)
    tiles = [slice(r, r + sub) for r in range(0, tm, sub)]
    chunks = [slice(rs.start // SSM_CHUNK, rs.stop // SSM_CHUNK) for rs in tiles]
    yg = [jnp.concatenate([yg_ref[o, cs].reshape(sub, LANES) for o in range(yg_ref.shape[0])], axis=1)
          for cs in chunks]
    glu = [mm(v.astype(BF16), wglu_ref[...]) + bglu_ref[...] for v in yg]
    ya_in = [(v * jax.nn.sigmoid(s)).astype(BF16) for v, s in zip(yg, glu)]
    ya = [mm(v, wa_ref[...]) for v in ya_in]
    yb = [mm(at_ref[0, rs], wb_ref[...]) for rs in tiles]
    mixed = [(ga_ref[0, rs].astype(F32) * a + gb_ref[0, rs].astype(F32) * b).astype(BF16)
             for rs, a, b in zip(tiles, ya, yb)]
    mo = [mm(v, wo_ref[...]) for v in mixed]
    xs = [x_ref[0, rs] + _rms(v) * gmix_ref[...] for rs, v in zip(tiles, mo)]

    hm = [(_rms(x) * gpre_ref[...]).astype(BF16) for x in xs]
    f = [None] * len(tiles)
    for c in range(0, dff, FF_CHUNK):
        hid = [mm(h, w1_ref[:, c:c + FF_CHUNK]) for h in hm]
        hid = [jnp.square(jnp.maximum(h, 0.0)).astype(BF16) for h in hid]
        part = [mm(h, w2_ref[c:c + FF_CHUNK, :]) for h in hid]
        f = [v if acc is None else acc + v for acc, v in zip(f, part)]
    xs = [x + _rms(v) * gpost_ref[...] for x, v in zip(xs, f)]

    e = [mm(p_ref[0, rs].astype(BF16), wple_ref[...]) for rs in tiles]
    gate = [mm(x.astype(BF16), wpg_ref[...]) for x in xs]
    for rs, x, ev, gv in zip(tiles, xs, e, gate):
        o_ref[0, rs] = x + _rms(ev * jax.nn.sigmoid(gv)) * gple_ref[...]


def _post(yg5, at, ga, gb, x, p, consts):
    b, s, d = x.shape
    tm = min(POST_TILE, s)
    tok = lambda w: pl.BlockSpec((1, tm, w), lambda bi, ti: (bi, ti, 0))
    return pl.pallas_call(
        _post_kernel,
        grid=(b, s // tm),
        in_specs=[_chunk_tile_spec(yg5.shape[0], tm), tok(at.shape[2]), tok(d), tok(d), tok(d),
                  tok(p.shape[2])] + [_const_spec(c.shape) for c in consts],
        out_specs=tok(d),
        out_shape=jax.ShapeDtypeStruct((b, s, d), F32),
        compiler_params=_params(("parallel", "parallel")),
        name="post",
    )(yg5, at, ga, gb, x, p, *consts)


def kernel(x, p, g_pre_mix, w_in, ssm_lam_re, ssm_lam_im, ssm_log_dt, ssm_b_re, ssm_b_im, ssm_c_re,
           ssm_c_im, ssm_d, w_glu, b_glu, w_branch_a, w_branch_b, w_out, g_post_mix, g_pre_mlp,
           w_mlp1, w_mlp2, g_post_mlp, w_ple, w_ple_gate, g_ple):
    b, s, d = x.shape
    depth = w_in.shape[0]
    wdt = w_glu.shape[1]
    a = w_branch_b.shape[1]
    hd = a // ATTN_HEADS
    row = lambda v: v.reshape(1, -1)
    for i in range(depth):
        w = w_in[i].astype(BF16)
        o_q, o_k, o_v, o_g = wdt, wdt + a, wdt + 2 * a, wdt + 3 * a
        u5, qt, k, vt, ga, gb = _in_proj(
            x, row(g_pre_mix[i]), w[:, :o_q], w[:, o_q:o_k].T, w[:, o_k:o_v], w[:, o_v:o_g].T,
            w[:, o_g:], float(hd) ** -0.5 * LOG2E)

        prep = _ssm_prep(ssm_lam_re[i], ssm_lam_im[i], ssm_log_dt[i], ssm_b_re[i], ssm_b_im[i],
                         ssm_c_re[i], ssm_c_im[i])
        dvec = jnp.tile(ssm_d[i].reshape(-1, 1, LANES), (1, 1, SSM_CHUNK))
        yg5 = _s5(u5, *prep, dvec)

        at = _moba(qt, k, vt)

        x = _post(yg5, at, ga, gb, x, p[i],
                  [w_glu[i].astype(BF16), row(b_glu[i]), w_branch_a[i].astype(BF16),
                   w_branch_b[i].astype(BF16), w_out[i].astype(BF16), row(g_post_mix[i]),
                   row(g_pre_mlp[i]), w_mlp1[i].astype(BF16), w_mlp2[i].astype(BF16),
                   row(g_post_mlp[i]), w_ple[i].astype(BF16), w_ple_gate[i].astype(BF16),
                   row(g_ple[i])])
    return x
```

```python
import functools

import jax
import jax.numpy as jnp
from jax import lax
from jax.experimental import pallas as pl
from jax.experimental.pallas import tpu as pltpu

F32 = jnp.float32
BF16 = jnp.bfloat16

SSM_GROUP = 16
SSM_STATE = 64
ATTN_HEADS = 8
MOBA_BLOCK = 256
MOBA_TOPK = 3
RMS_EPS = 1e-6
NEG_INF = -1e30
LOG2E = 1.4426950408889634

LANES = 128
F32_SUBLANES = 8
VMEM_LIMIT = 48 * 1024 * 1024

SSM_CHUNK = F32_SUBLANES
TILE_GROUPS = LANES // SSM_GROUP
SCAN_CHUNKS = 32
S5_SUB_BLOCKS = 4
HEAD_PAIR = 2
MOBA_STEP_HEADS = 4
SUM_ROWS = 16
SCORE_LOOKAHEAD = 2
VALUE_LAG = 2
TOKEN_TILE = 1024
POST_TILE = 512
ROW_SUB = 256
FF_CHUNK = 1024

_NT = (((1,), (1,)), ((), ()))


def _rms(v):
    return v * lax.rsqrt(jnp.mean(v * v, axis=-1, keepdims=True) + RMS_EPS)


def _const_spec(shape):
    nd = len(shape)
    return pl.BlockSpec(shape, lambda *_: (0,) * nd, pipeline_mode=pl.Buffered(1))


def _params(sem):
    return pltpu.CompilerParams(dimension_semantics=sem, vmem_limit_bytes=VMEM_LIMIT)


def _chunk_tile_spec(n_tiles, tm):
    return pl.BlockSpec((n_tiles, tm // SSM_CHUNK, None, SSM_CHUNK, LANES),
                        lambda bi, ti: (0, ti, bi, 0, 0))


def _in_proj_kernel(x_ref, g_ref, wu_ref, wqt_ref, wk_ref, wvt_ref, wg_ref,
                    u_ref, qt_ref, k_ref, vt_ref, ga_ref, gb_ref, *, q_scale):
    tm, d = x_ref.shape[1], x_ref.shape[2]
    sub = min(ROW_SUB, tm)
    tiles = [slice(r, r + sub) for r in range(0, tm, sub)]
    norm = lambda rs: (_rms(x_ref[0, rs]) * g_ref[...]).astype(BF16)
    hbs = [norm(tiles[0])]
    for n, rs in enumerate(tiles):
        hb = hbs[n]
        if n + 1 < len(tiles):
            hbs.append(norm(tiles[n + 1]))
        cs = slice(rs.start // SSM_CHUNK, rs.stop // SSM_CHUNK)
        u = jnp.dot(hb, wu_ref[...], preferred_element_type=F32)
        for o in range(u_ref.shape[0]):
            u_ref[o, cs] = u[:, o * LANES:(o + 1) * LANES].reshape(sub // SSM_CHUNK, SSM_CHUNK, LANES)
        k_ref[0, rs] = jnp.dot(hb, wk_ref[...], preferred_element_type=F32).astype(BF16)
        qt = lax.dot_general(wqt_ref[...], hb, _NT, preferred_element_type=F32)
        qt_ref[0, :, rs] = (qt * q_scale).astype(BF16)
        vt_ref[0, :, rs] = lax.dot_general(wvt_ref[...], hb, _NT,
                                           preferred_element_type=F32).astype(BF16)
        gates = jax.nn.sigmoid(jnp.dot(hb, wg_ref[...], preferred_element_type=F32))
        ga_ref[0, rs] = gates[:, :d].astype(BF16)
        gb_ref[0, rs] = gates[:, d:].astype(BF16)


def _in_proj(x, g, wu, wqt, wk, wvt, wg, q_scale):
    b, s, d = x.shape
    n_tiles = wu.shape[1] // LANES
    a = wk.shape[1]
    tm = min(TOKEN_TILE, s)
    tok = lambda w: pl.BlockSpec((1, tm, w), lambda bi, ti: (bi, ti, 0))
    tr = lambda w: pl.BlockSpec((1, w, tm), lambda bi, ti: (bi, 0, ti))
    return pl.pallas_call(
        functools.partial(_in_proj_kernel, q_scale=q_scale),
        grid=(b, s // tm),
        in_specs=[tok(d), _const_spec(g.shape), _const_spec(wu.shape), _const_spec(wqt.shape),
                  _const_spec(wk.shape), _const_spec(wvt.shape), _const_spec(wg.shape)],
        out_specs=[_chunk_tile_spec(n_tiles, tm), tr(a), tok(a), tr(a), tok(d), tok(d)],
        out_shape=[jax.ShapeDtypeStruct((n_tiles, s // SSM_CHUNK, b, SSM_CHUNK, LANES), F32),
                   jax.ShapeDtypeStruct((b, a, s), BF16), jax.ShapeDtypeStruct((b, s, a), BF16),
                   jax.ShapeDtypeStruct((b, a, s), BF16),
                   jax.ShapeDtypeStruct((b, s, d), BF16), jax.ShapeDtypeStruct((b, s, d), BF16)],
        compiler_params=_params(("parallel", "parallel")),
        name="in_proj",
    )(x, g, wu, wqt, wk, wvt, wg)


def _ssm_prep_kernel(lre_ref, lim_ref, ldt_ref, btr_ref, bti_ref, cr_ref, ci_ref,
                     toep_ref, wst_ref, wout_ref, ltr_ref, lti_ref):
    t = SSM_CHUNK
    rows, width = cr_ref.shape[1], cr_ref.shape[2]
    lre = lre_ref[0]
    lim = lim_ref[0]
    dt = jnp.exp(ldt_ref[0])
    ea = lre * dt
    eb = lim * dt

    def power(tau):
        mag = jnp.exp(tau * ea)
        return mag * jnp.cos(tau * eb), mag * jnp.sin(tau * eb)

    def times(pw, mr, mi):
        return pw[0] * mr - pw[1] * mi, pw[0] * mi + pw[1] * mr

    pw = [power(float(tau)) for tau in range(t + 1)]
    ltr_ref[0], lti_ref[0] = pw[t]
    same_group = (lax.broadcasted_iota(jnp.int32, (rows, width), 0) // SSM_GROUP
                  == lax.broadcasted_iota(jnp.int32, (rows, width), 1) // SSM_STATE)
    keep = lambda ref: jnp.where(same_group, ref[0], 0.0)
    cr, ci = keep(cr_ref), keep(ci_ref)
    nr = pw[1][0] - 1.0
    ni = pw[1][1]
    den = lre * lre + lim * lim
    coef = ((nr * lre + ni * lim) / den, (ni * lre - nr * lim) / den)
    bbr, bbi = times(coef, keep(btr_ref), keep(bti_ref))

    for s in range(t):
        wr, wi = times(pw[t - 1 - s], bbr, bbi)
        wst_ref[0, s * rows:(s + 1) * rows, :] = jnp.concatenate([wr, wi], axis=1).astype(BF16)
        orr, oi = times(pw[s + 1], cr, ci)
        wout_ref[0, :width, s * rows:(s + 1) * rows] = orr.T.astype(BF16)
        wout_ref[0, width:, s * rows:(s + 1) * rows] = (-oi).T.astype(BF16)

    cl = [times(pw[tau], cr, ci) for tau in range(t)]
    clr = jnp.concatenate([m[0] for m in cl], axis=0)
    cli = jnp.concatenate([m[1] for m in cl], axis=0)
    hi = lax.Precision.HIGHEST
    bd = (lax.dot_general(bbr, clr, _NT, precision=hi, preferred_element_type=F32)
          - lax.dot_general(bbi, cli, _NT, precision=hi, preferred_element_type=F32)).astype(BF16)
    for s in range(t):
        lead = [jnp.zeros((rows, s * rows), BF16)] if s else []
        toep_ref[0, s * rows:(s + 1) * rows, :] = jnp.concatenate(
            lead + [bd[:, :(t - s) * rows]], axis=1)


def _ssm_prep(lam_re, lam_im, log_dt, b_re, b_im, c_re, c_im):
    g, p = lam_re.shape
    tg, t = TILE_GROUPS, SSM_CHUNK
    n = g // tg
    width = tg * p
    lanes = lambda v: v.reshape(n, 1, width)
    chan = lambda m: jnp.tile(m.reshape(n, LANES, p), (1, 1, tg))
    vec = pl.BlockSpec((1, 1, width), lambda ti: (ti, 0, 0))
    mat = pl.BlockSpec((1, LANES, width), lambda ti: (ti, 0, 0))
    sq = lambda r, c: pl.BlockSpec((1, r, c), lambda ti: (ti, 0, 0))
    return pl.pallas_call(
        _ssm_prep_kernel,
        grid=(n,),
        in_specs=[vec, vec, vec, mat, mat, mat, mat],
        out_specs=[sq(t * LANES, t * LANES), sq(t * LANES, 2 * width), sq(2 * width, t * LANES),
                   vec, vec],
        out_shape=[jax.ShapeDtypeStruct((n, t * LANES, t * LANES), BF16),
                   jax.ShapeDtypeStruct((n, t * LANES, 2 * width), BF16),
                   jax.ShapeDtypeStruct((n, 2 * width, t * LANES), BF16),
                   jax.ShapeDtypeStruct((n, 1, width), F32), jax.ShapeDtypeStruct((n, 1, width), F32)],
        compiler_params=_params(("parallel",)),
        name="ssm_prep",
    )(lanes(lam_re), lanes(lam_im), lanes(jnp.repeat(log_dt, p)),
      chan(jnp.swapaxes(b_re, 1, 2)), chan(jnp.swapaxes(b_im, 1, 2)), chan(c_re), chan(c_im))


def _s5_kernel(u_ref, m_ref, wst_ref, wout_ref, lr_ref, li_ref, d_ref, y_ref, h_ref, st_ref, *, nb):
    t = SSM_CHUNK
    rows = u_ref.shape[0] // t
    half = lr_ref.shape[-1]

    @pl.when(pl.program_id(1) == 0)
    def _():
        st_ref[...] = jnp.zeros(st_ref.shape, F32)

    rp = rows // S5_SUB_BLOCKS
    parts = [slice(p * rp, (p + 1) * rp) for p in range(S5_SUB_BLOCKS)]
    u32 = [jnp.concatenate([u_ref[pl.ds(rs.start * t + s, rp, stride=t), :] for s in range(t)], axis=1)
           for rs in parts]
    u = [v.astype(BF16) for v in u32]
    for rs, v in zip(parts, u):
        h_ref[rs, :] = jnp.dot(v, wst_ref[...], preferred_element_type=F32)
    lam_r = lr_ref[...]
    lam_i = li_ref[...]

    def step(k, carry):
        sr, si = carry
        rk = pl.ds(pl.multiple_of(k * nb, nb), nb)
        hr = h_ref[rk, :half]
        hi = h_ref[rk, half:]
        h_ref[rk, :half] = sr
        h_ref[rk, half:] = si
        return (lam_r * sr - lam_i * si + hr, lam_r * si + lam_i * sr + hi)

    y = [jnp.dot(v, m_ref[...], preferred_element_type=F32) + d_ref[...] * v32
         for v, v32 in zip(u, u32)]
    sr, si = lax.fori_loop(0, rows // nb, step, (st_ref[0], st_ref[1]), unroll=True)
    st_ref[0] = sr
    st_ref[1] = si
    y = [v + jnp.dot(h_ref[rs, :].astype(BF16), wout_ref[...], preferred_element_type=F32)
         for rs, v in zip(parts, y)]
    for rs, v in zip(parts, y):
        v = jax.nn.gelu(v)
        for s in range(t):
            y_ref[pl.ds(rs.start * t + s, rp, stride=t), :] = v[:, s * LANES:(s + 1) * LANES]


def _s5(u5, toep, wst, wout, lam_r, lam_i, dvec):
    n_tiles, n_chunks, nb, t, lanes = u5.shape
    kb = min(SCAN_CHUNKS, n_chunks)
    blk_tokens = kb * nb * t
    flat = u5.reshape(n_tiles, n_chunks * nb * t, lanes)
    width = toep.shape[-1]
    st = wst.shape[-1]
    tok = pl.BlockSpec((None, blk_tokens, lanes), lambda oi, ki: (oi, ki, 0))
    op = lambda r, c: pl.BlockSpec((None, r, c), lambda oi, ki: (oi, 0, 0))
    y = pl.pallas_call(
        functools.partial(_s5_kernel, nb=nb),
        grid=(n_tiles, n_chunks // kb),
        in_specs=[tok, op(width, width), op(width, st), op(st, width), op(1, st // 2), op(1, st // 2),
                  op(1, width)],
        out_specs=tok,
        out_shape=jax.ShapeDtypeStruct(flat.shape, F32),
        scratch_shapes=[pltpu.VMEM((kb * nb, st), F32), pltpu.VMEM((2, nb, st // 2), F32)],
        compiler_params=_params(("parallel", "arbitrary")),
        name="s5",
    )(flat, toep, wst, wout, lam_r, lam_i, dvec)
    return y.reshape(u5.shape)


def _moba_kernel(qt_ref, k_ref, vt_ref, o_ref, *, n_blk, n_sel):
    blk = MOBA_BLOCK
    lanes = LANES
    hd = lanes // HEAD_PAIR
    n_heads = qt_ref.shape[1] // hd
    pair_lanes = lambda h: slice((h // HEAD_PAIR) * lanes, (h // HEAD_PAIR + 1) * lanes)
    gate_rows = 16
    means = [jnp.mean(k_ref[0, j * blk:(j + 1) * blk, :].astype(F32), axis=0, keepdims=True)
             for j in range(n_blk)]
    means.append(jnp.zeros((gate_rows - n_blk, k_ref.shape[2]), F32))
    kmean = jnp.concatenate(means, axis=0).astype(BF16)
    head_row = lax.broadcasted_iota(jnp.int32, (lanes, blk), 0) // hd
    blk_row = lax.broadcasted_iota(jnp.int32, (gate_rows, blk), 0)
    causal = (lax.broadcasted_iota(jnp.int32, (blk, blk), 0)
              <= lax.broadcasted_iota(jnp.int32, (blk, blk), 1))
    ones = jnp.ones((SUM_ROWS, k_ref.shape[1]), BF16)
    v_ext = [jnp.concatenate([vt_ref[0, h * hd:(h + 1) * hd, :], ones], axis=0)
             for h in range(n_heads)]

    def scores(i, h):
        qp = qt_ref[0, pair_lanes(h), i * blk:(i + 1) * blk]
        qh = jnp.where(head_row == h % HEAD_PAIR, qp, jnp.zeros_like(qp))
        s = jnp.dot(k_ref[0, 0:(i + 1) * blk, pair_lanes(h)], qh, preferred_element_type=F32)
        bias = [None] * i
        if i > n_sel:
            gate = jnp.dot(kmean[:, pair_lanes(h)], qh, preferred_element_type=F32)
            for n in range(i):
                gn = gate[n:n + 1, :]
                ahead = (gate > gn) | ((gate == gn) & (blk_row < n))
                ahead = ahead & (blk_row < i)
                rank = jnp.sum(ahead.astype(F32), axis=0, keepdims=True)
                bias[n] = jnp.where(rank < n_sel, 0.0, NEG_INF)
        return s, bias

    def attend(i, s, bias):
        parts = []
        tops = []
        for j in range(i + 1):
            sj = s[j * blk:(j + 1) * blk, :]
            if j == i:
                sj = jnp.where(causal, sj, NEG_INF)
            top = jnp.max(sj, axis=0, keepdims=True)
            if j < i and bias[j] is not None:
                top = top + bias[j]
            parts.append(sj)
            tops.append(top)
        m = functools.reduce(jnp.maximum, tops)
        probs = []
        for j in range(i + 1):
            shift = m - bias[j] if (j < i and bias[j] is not None) else m
            probs.append(jnp.exp2(parts[j] - shift).astype(BF16))
        return jnp.concatenate(probs, axis=0) if i > 0 else probs[0]

    outs = []

    def values(i, h, pt):
        o = jnp.dot(v_ext[h][:, 0:(i + 1) * blk], pt, preferred_element_type=F32)
        outs.append(o[:hd] / o[hd:hd + 1])
        if h % HEAD_PAIR == HEAD_PAIR - 1:
            ot = jnp.concatenate(outs, axis=0)
            o_ref[0, i * blk:(i + 1) * blk, pair_lanes(h)] = ot.T.astype(BF16)
            outs.clear()

    units = [(i, h) for i in reversed(range(n_blk)) for h in range(n_heads)]
    pending = [scores(*u) for u in units[:SCORE_LOOKAHEAD]]
    probs_done = []
    for n, (i, h) in enumerate(units):
        cur = pending.pop(0)
        if n + SCORE_LOOKAHEAD < len(units):
            pending.append(scores(*units[n + SCORE_LOOKAHEAD]))
        probs_done.append((i, h, attend(i, *cur)))
        if len(probs_done) > VALUE_LAG:
            values(*probs_done.pop(0))
    for item in probs_done:
        values(*item)


def _moba(qt, k, vt):
    b, a, s = qt.shape
    lanes = MOBA_STEP_HEADS * (a // ATTN_HEADS)
    n_blk = s // MOBA_BLOCK
    tr = pl.BlockSpec((1, lanes, s), lambda bi, hi: (bi, hi, 0))
    tok = pl.BlockSpec((1, s, lanes), lambda bi, hi: (bi, 0, hi))
    return pl.pallas_call(
        functools.partial(_moba_kernel, n_blk=n_blk, n_sel=min(MOBA_TOPK, n_blk - 1)),
        grid=(b, a // lanes),
        in_specs=[tr, tok, tr],
        out_specs=tok,
        out_shape=jax.ShapeDtypeStruct((b, s, a), BF16),
        compiler_params=_params(("parallel", "parallel")),
        name="moba",
    )(qt, k, vt)


def _post_kernel(yg_ref, at_ref, ga_ref, gb_ref, x_ref, p_ref, wglu_ref, bglu_ref, wa_ref, wb_ref,
                 wo_ref, gmix_ref, gpre_ref, w1_ref, w2_ref, gpost_ref, wple_ref, wpg_ref, gple_ref,
                 o_ref):
    tm = x_ref.shape[1]
    sub = min(ROW_SUB, tm)
    dff = w1_ref.shape[1]
    mm = lambda a, w: jnp.dot(a, w, preferred_element_type=F32)
    tiles = [slice(r, r + sub) for r in range(0, tm, sub)]
    chunks = [slice(rs.start // SSM_CHUNK, rs.stop // SSM_CHUNK) for rs in tiles]
    yg = [jnp.concatenate([yg_ref[o, cs].reshape(sub, LANES) for o in range(yg_ref.shape[0])], axis=1)
          for cs in chunks]
    glu = [mm(v.astype(BF16), wglu_ref[...]) + bglu_ref[...] for v in yg]
    yb = [mm(at_ref[0, rs], wb_ref[...]) for rs in tiles]
    ya_in = [(v * jax.nn.sigmoid(s)).astype(BF16) for v, s in zip(yg, glu)]
    ya = [mm(v, wa_ref[...]) for v in ya_in]
    mixed = [(ga_ref[0, rs].astype(F32) * a + gb_ref[0, rs].astype(F32) * b).astype(BF16)
             for rs, a, b in zip(tiles, ya, yb)]
    mo = [mm(v, wo_ref[...]) for v in mixed]
    e = [mm(p_ref[0, rs].astype(BF16), wple_ref[...]) for rs in tiles]
    xs = [x_ref[0, rs] + _rms(v) * gmix_ref[...] for rs, v in zip(tiles, mo)]

    hm = [(_rms(x) * gpre_ref[...]).astype(BF16) for x in xs]
    f = [None] * len(tiles)
    for c in range(0, dff, FF_CHUNK):
        hid = [mm(h, w1_ref[:, c:c + FF_CHUNK]) for h in hm]
        hid = [jnp.square(jnp.maximum(h, 0.0)).astype(BF16) for h in hid]
        part = [mm(h, w2_ref[c:c + FF_CHUNK, :]) for h in hid]
        f = [v if acc is None else acc + v for acc, v in zip(f, part)]
    xs = [x + _rms(v) * gpost_ref[...] for x, v in zip(xs, f)]

    gate = [mm(x.astype(BF16), wpg_ref[...]) for x in xs]
    for rs, x, ev, gv in zip(tiles, xs, e, gate):
        o_ref[0, rs] = x + _rms(ev * jax.nn.sigmoid(gv)) * gple_ref[...]


def _post(yg5, at, ga, gb, x, p, consts):
    b, s, d = x.shape
    tm = min(POST_TILE, s)
    tok = lambda w: pl.BlockSpec((1, tm, w), lambda bi, ti: (bi, ti, 0))
    return pl.pallas_call(
        _post_kernel,
        grid=(b, s // tm),
        in_specs=[_chunk_tile_spec(yg5.shape[0], tm), tok(at.shape[2]), tok(d), tok(d), tok(d),
                  tok(p.shape[2])] + [_const_spec(c.shape) for c in consts],
        out_specs=tok(d),
        out_shape=jax.ShapeDtypeStruct((b, s, d), F32),
        compiler_params=_params(("parallel", "parallel")),
        name="post",
    )(yg5, at, ga, gb, x, p, *consts)


def kernel(x, p, g_pre_mix, w_in, ssm_lam_re, ssm_lam_im, ssm_log_dt, ssm_b_re, ssm_b_im, ssm_c_re,
           ssm_c_im, ssm_d, w_glu, b_glu, w_branch_a, w_branch_b, w_out, g_post_mix, g_pre_mlp,
           w_mlp1, w_mlp2, g_post_mlp, w_ple, w_ple_gate, g_ple):
    b, s, d = x.shape
    depth = w_in.shape[0]
    wdt = w_glu.shape[1]
    a = w_branch_b.shape[1]
    hd = a // ATTN_HEADS
    assert hd * HEAD_PAIR == LANES and ATTN_HEADS % MOBA_STEP_HEADS == 0
    assert wdt % LANES == 0 and s % MOBA_BLOCK == 0 and s % min(TOKEN_TILE, s) == 0
    assert (s // SSM_CHUNK) % min(SCAN_CHUNKS, s // SSM_CHUNK) == 0
    row = lambda v: v.reshape(1, -1)
    for i in range(depth):
        w = w_in[i].astype(BF16)
        o_q, o_k, o_v, o_g = wdt, wdt + a, wdt + 2 * a, wdt + 3 * a
        u5, qt, k, vt, ga, gb = _in_proj(
            x, row(g_pre_mix[i]), w[:, :o_q], w[:, o_q:o_k].T, w[:, o_k:o_v], w[:, o_v:o_g].T,
            w[:, o_g:], float(hd) ** -0.5 * LOG2E)

        prep = _ssm_prep(ssm_lam_re[i], ssm_lam_im[i], ssm_log_dt[i], ssm_b_re[i], ssm_b_im[i],
                         ssm_c_re[i], ssm_c_im[i])
        dvec = jnp.tile(ssm_d[i].reshape(-1, 1, LANES), (1, 1, SSM_CHUNK))
        yg5 = _s5(u5, *prep, dvec)

        at = _moba(qt, k, vt)

        x = _post(yg5, at, ga, gb, x, p[i],
                  [w_glu[i].astype(BF16), row(b_glu[i]), w_branch_a[i].astype(BF16),
                   w_branch_b[i].astype(BF16), w_out[i].astype(BF16), row(g_post_mix[i]),
                   row(g_pre_mlp[i]), w_mlp1[i].astype(BF16), w_mlp2[i].astype(BF16),
                   row(g_post_mlp[i]), w_ple[i].astype(BF16), w_ple_gate[i].astype(BF16),
                   row(g_ple[i])])
    return x
```

```python
import functools

import jax
import jax.numpy as jnp
from jax import lax
from jax.experimental import pallas as pl
from jax.experimental.pallas import tpu as pltpu

F32 = jnp.float32
BF16 = jnp.bfloat16

SSM_GROUP = 16
SSM_STATE = 64
ATTN_HEADS = 8
MOBA_BLOCK = 256
MOBA_TOPK = 3
RMS_EPS = 1e-6
NEG_INF = -1e30
LOG2E = 1.4426950408889634

LANES = 128
F32_SUBLANES = 8
VMEM_LIMIT = 48 * 1024 * 1024

SSM_CHUNK = F32_SUBLANES
TILE_GROUPS = LANES // SSM_GROUP
SCAN_CHUNKS = 32
S5_SUB_BLOCKS = 4
HEAD_PAIR = 2
MOBA_STEP_HEADS = 4
SUM_ROWS = 16
SCORE_LOOKAHEAD = 2
FLASH_LOOKAHEAD = 5
VALUE_LAG = 2
TOKEN_TILE = 1024
POST_TILE = 512
ROW_SUB = 256
FF_CHUNK = 1024

_NT = (((1,), (1,)), ((), ()))


def _rms(v):
    return v * lax.rsqrt(jnp.mean(v * v, axis=-1, keepdims=True) + RMS_EPS)


def _const_spec(shape):
    nd = len(shape)
    return pl.BlockSpec(shape, lambda *_: (0,) * nd, pipeline_mode=pl.Buffered(1))


def _params(sem):
    return pltpu.CompilerParams(dimension_semantics=sem, vmem_limit_bytes=VMEM_LIMIT)


def _chunk_tile_spec(n_tiles, tm):
    return pl.BlockSpec((n_tiles, tm // SSM_CHUNK, None, SSM_CHUNK, LANES),
                        lambda bi, ti: (0, ti, bi, 0, 0))


def _in_proj_kernel(x_ref, g_ref, wu_ref, wqt_ref, wk_ref, wvt_ref, wg_ref,
                    u_ref, qt_ref, k_ref, vt_ref, ga_ref, gb_ref, *, q_scale):
    tm, d = x_ref.shape[1], x_ref.shape[2]
    sub = min(ROW_SUB, tm)
    tiles = [slice(r, r + sub) for r in range(0, tm, sub)]
    norm = lambda rs: (_rms(x_ref[0, rs]) * g_ref[...]).astype(BF16)
    hbs = [norm(tiles[0])]
    for n, rs in enumerate(tiles):
        hb = hbs[n]
        if n + 1 < len(tiles):
            hbs.append(norm(tiles[n + 1]))
        cs = slice(rs.start // SSM_CHUNK, rs.stop // SSM_CHUNK)
        u = jnp.dot(hb, wu_ref[...], preferred_element_type=F32)
        for o in range(u_ref.shape[0]):
            u_ref[o, cs] = u[:, o * LANES:(o + 1) * LANES].reshape(sub // SSM_CHUNK, SSM_CHUNK, LANES)
        k_ref[0, rs] = jnp.dot(hb, wk_ref[...], preferred_element_type=F32).astype(BF16)
        qt = lax.dot_general(wqt_ref[...], hb, _NT, preferred_element_type=F32)
        qt_ref[0, :, rs] = (qt * q_scale).astype(BF16)
        vt_ref[0, :, rs] = lax.dot_general(wvt_ref[...], hb, _NT,
                                           preferred_element_type=F32).astype(BF16)
        gates = jax.nn.sigmoid(jnp.dot(hb, wg_ref[...], preferred_element_type=F32))
        ga_ref[0, rs] = gates[:, :d].astype(BF16)
        gb_ref[0, rs] = gates[:, d:].astype(BF16)


def _in_proj(x, g, wu, wqt, wk, wvt, wg, q_scale):
    b, s, d = x.shape
    n_tiles = wu.shape[1] // LANES
    a = wk.shape[1]
    tm = min(TOKEN_TILE, s)
    tok = lambda w: pl.BlockSpec((1, tm, w), lambda bi, ti: (bi, ti, 0))
    tr = lambda w: pl.BlockSpec((1, w, tm), lambda bi, ti: (bi, 0, ti))
    return pl.pallas_call(
        functools.partial(_in_proj_kernel, q_scale=q_scale),
        grid=(b, s // tm),
        in_specs=[tok(d), _const_spec(g.shape), _const_spec(wu.shape), _const_spec(wqt.shape),
                  _const_spec(wk.shape), _const_spec(wvt.shape), _const_spec(wg.shape)],
        out_specs=[_chunk_tile_spec(n_tiles, tm), tr(a), tok(a), tr(a), tok(d), tok(d)],
        out_shape=[jax.ShapeDtypeStruct((n_tiles, s // SSM_CHUNK, b, SSM_CHUNK, LANES), F32),
                   jax.ShapeDtypeStruct((b, a, s), BF16), jax.ShapeDtypeStruct((b, s, a), BF16),
                   jax.ShapeDtypeStruct((b, a, s), BF16),
                   jax.ShapeDtypeStruct((b, s, d), BF16), jax.ShapeDtypeStruct((b, s, d), BF16)],
        compiler_params=_params(("parallel", "parallel")),
        name="in_proj",
    )(x, g, wu, wqt, wk, wvt, wg)


def _ssm_prep_kernel(lre_ref, lim_ref, ldt_ref, btr_ref, bti_ref, cr_ref, ci_ref,
                     toep_ref, wst_ref, wout_ref, ltr_ref, lti_ref):
    t = SSM_CHUNK
    rows, width = cr_ref.shape[1], cr_ref.shape[2]
    lre = lre_ref[0]
    lim = lim_ref[0]
    dt = jnp.exp(ldt_ref[0])
    ea = lre * dt
    eb = lim * dt

    def power(tau):
        mag = jnp.exp(tau * ea)
        return mag * jnp.cos(tau * eb), mag * jnp.sin(tau * eb)

    def times(pw, mr, mi):
        return pw[0] * mr - pw[1] * mi, pw[0] * mi + pw[1] * mr

    pw = [power(float(tau)) for tau in range(t + 1)]
    ltr_ref[0], lti_ref[0] = pw[t]
    same_group = (lax.broadcasted_iota(jnp.int32, (rows, width), 0) // SSM_GROUP
                  == lax.broadcasted_iota(jnp.int32, (rows, width), 1) // SSM_STATE)
    keep = lambda ref: jnp.where(same_group, ref[0], 0.0)
    cr, ci = keep(cr_ref), keep(ci_ref)
    nr = pw[1][0] - 1.0
    ni = pw[1][1]
    den = lre * lre + lim * lim
    coef = ((nr * lre + ni * lim) / den, (ni * lre - nr * lim) / den)
    bbr, bbi = times(coef, keep(btr_ref), keep(bti_ref))

    for s in range(t):
        wr, wi = times(pw[t - 1 - s], bbr, bbi)
        wst_ref[0, s * rows:(s + 1) * rows, :] = jnp.concatenate([wr, wi], axis=1).astype(BF16)
        orr, oi = times(pw[s + 1], cr, ci)
        wout_ref[0, :width, s * rows:(s + 1) * rows] = orr.T.astype(BF16)
        wout_ref[0, width:, s * rows:(s + 1) * rows] = (-oi).T.astype(BF16)

    cl = [times(pw[tau], cr, ci) for tau in range(t)]
    clr = jnp.concatenate([m[0] for m in cl], axis=0)
    cli = jnp.concatenate([m[1] for m in cl], axis=0)
    hi = lax.Precision.HIGHEST
    bd = (lax.dot_general(bbr, clr, _NT, precision=hi, preferred_element_type=F32)
          - lax.dot_general(bbi, cli, _NT, precision=hi, preferred_element_type=F32)).astype(BF16)
    for s in range(t):
        lead = [jnp.zeros((rows, s * rows), BF16)] if s else []
        toep_ref[0, s * rows:(s + 1) * rows, :] = jnp.concatenate(
            lead + [bd[:, :(t - s) * rows]], axis=1)


def _ssm_prep(lam_re, lam_im, log_dt, b_re, b_im, c_re, c_im):
    g, p = lam_re.shape
    tg, t = TILE_GROUPS, SSM_CHUNK
    n = g // tg
    width = tg * p
    lanes = lambda v: v.reshape(n, 1, width)
    chan = lambda m: jnp.tile(m.reshape(n, LANES, p), (1, 1, tg))
    vec = pl.BlockSpec((1, 1, width), lambda ti: (ti, 0, 0))
    mat = pl.BlockSpec((1, LANES, width), lambda ti: (ti, 0, 0))
    sq = lambda r, c: pl.BlockSpec((1, r, c), lambda ti: (ti, 0, 0))
    return pl.pallas_call(
        _ssm_prep_kernel,
        grid=(n,),
        in_specs=[vec, vec, vec, mat, mat, mat, mat],
        out_specs=[sq(t * LANES, t * LANES), sq(t * LANES, 2 * width), sq(2 * width, t * LANES),
                   vec, vec],
        out_shape=[jax.ShapeDtypeStruct((n, t * LANES, t * LANES), BF16),
                   jax.ShapeDtypeStruct((n, t * LANES, 2 * width), BF16),
                   jax.ShapeDtypeStruct((n, 2 * width, t * LANES), BF16),
                   jax.ShapeDtypeStruct((n, 1, width), F32), jax.ShapeDtypeStruct((n, 1, width), F32)],
        compiler_params=_params(("parallel",)),
        name="ssm_prep",
    )(lanes(lam_re), lanes(lam_im), lanes(jnp.repeat(log_dt, p)),
      chan(jnp.swapaxes(b_re, 1, 2)), chan(jnp.swapaxes(b_im, 1, 2)), chan(c_re), chan(c_im))


def _s5_kernel(u_ref, m_ref, wst_ref, wout_ref, lr_ref, li_ref, d_ref, y_ref, h_ref, st_ref, *, nb):
    t = SSM_CHUNK
    rows = u_ref.shape[0] // t
    half = lr_ref.shape[-1]

    @pl.when(pl.program_id(1) == 0)
    def _():
        st_ref[...] = jnp.zeros(st_ref.shape, F32)

    rp = rows // S5_SUB_BLOCKS
    parts = [slice(p * rp, (p + 1) * rp) for p in range(S5_SUB_BLOCKS)]
    u32 = [jnp.concatenate([u_ref[pl.ds(rs.start * t + s, rp, stride=t), :] for s in range(t)], axis=1)
           for rs in parts]
    u = [v.astype(BF16) for v in u32]
    for rs, v in zip(parts, u):
        h_ref[rs, :] = jnp.dot(v, wst_ref[...], preferred_element_type=F32)
    lam_r = lr_ref[...]
    lam_i = li_ref[...]

    def step(k, carry):
        sr, si = carry
        rk = pl.ds(pl.multiple_of(k * nb, nb), nb)
        hr = h_ref[rk, :half]
        hi = h_ref[rk, half:]
        h_ref[rk, :half] = sr
        h_ref[rk, half:] = si
        return (lam_r * sr - lam_i * si + hr, lam_r * si + lam_i * sr + hi)

    y = [jnp.dot(v, m_ref[...], preferred_element_type=F32) + d_ref[...] * v32
         for v, v32 in zip(u, u32)]
    sr, si = lax.fori_loop(0, rows // nb, step, (st_ref[0], st_ref[1]), unroll=True)
    st_ref[0] = sr
    st_ref[1] = si
    y = [v + jnp.dot(h_ref[rs, :].astype(BF16), wout_ref[...], preferred_element_type=F32)
         for rs, v in zip(parts, y)]
    for rs, v in zip(parts, y):
        v = jax.nn.gelu(v)
        for s in range(t):
            y_ref[pl.ds(rs.start * t + s, rp, stride=t), :] = v[:, s * LANES:(s + 1) * LANES]


def _s5(u5, toep, wst, wout, lam_r, lam_i, dvec):
    n_tiles, n_chunks, nb, t, lanes = u5.shape
    kb = min(SCAN_CHUNKS, n_chunks)
    blk_tokens = kb * nb * t
    flat = u5.reshape(n_tiles, n_chunks * nb * t, lanes)
    width = toep.shape[-1]
    st = wst.shape[-1]
    tok = pl.BlockSpec((None, blk_tokens, lanes), lambda oi, ki: (oi, ki, 0))
    op = lambda r, c: pl.BlockSpec((None, r, c), lambda oi, ki: (oi, 0, 0))
    y = pl.pallas_call(
        functools.partial(_s5_kernel, nb=nb),
        grid=(n_tiles, n_chunks // kb),
        in_specs=[tok, op(width, width), op(width, st), op(st, width), op(1, st // 2), op(1, st // 2),
                  op(1, width)],
        out_specs=tok,
        out_shape=jax.ShapeDtypeStruct(flat.shape, F32),
        scratch_shapes=[pltpu.VMEM((kb * nb, st), F32), pltpu.VMEM((2, nb, st // 2), F32)],
        compiler_params=_params(("parallel", "arbitrary")),
        name="s5",
    )(flat, toep, wst, wout, lam_r, lam_i, dvec)
    return y.reshape(u5.shape)


def _moba_kernel(qt_ref, k_ref, vt_ref, o_ref, *, n_blk, n_sel):
    blk = MOBA_BLOCK
    lanes = LANES
    hd = lanes // HEAD_PAIR
    n_heads = qt_ref.shape[1] // hd
    pair_lanes = lambda h: slice((h // HEAD_PAIR) * lanes, (h // HEAD_PAIR + 1) * lanes)
    gate_rows = 16
    means = [jnp.mean(k_ref[0, j * blk:(j + 1) * blk, :].astype(F32), axis=0, keepdims=True)
             for j in range(n_blk)]
    means.append(jnp.zeros((gate_rows - n_blk, k_ref.shape[2]), F32))
    kmean = jnp.concatenate(means, axis=0).astype(BF16)
    head_row = lax.broadcasted_iota(jnp.int32, (lanes, blk), 0) // hd
    blk_row = lax.broadcasted_iota(jnp.int32, (gate_rows, blk), 0)
    causal = (lax.broadcasted_iota(jnp.int32, (blk, blk), 0)
              <= lax.broadcasted_iota(jnp.int32, (blk, blk), 1))
    ones = jnp.ones((SUM_ROWS, k_ref.shape[1]), BF16)
    v_ext = [jnp.concatenate([vt_ref[0, h * hd:(h + 1) * hd, :], ones], axis=0)
             for h in range(n_heads)]

    def scores(i, h):
        qp = qt_ref[0, pair_lanes(h), i * blk:(i + 1) * blk]
        qh = jnp.where(head_row == h % HEAD_PAIR, qp, jnp.zeros_like(qp))
        s = jnp.dot(k_ref[0, 0:(i + 1) * blk, pair_lanes(h)], qh, preferred_element_type=F32)
        bias = [None] * i
        if i > n_sel:
            gate = jnp.dot(kmean[:, pair_lanes(h)], qh, preferred_element_type=F32)
            for n in range(i):
                gn = gate[n:n + 1, :]
                ahead = (gate > gn) | ((gate == gn) & (blk_row < n))
                ahead = ahead & (blk_row < i)
                rank = jnp.sum(ahead.astype(F32), axis=0, keepdims=True)
                bias[n] = jnp.where(rank < n_sel, 0.0, NEG_INF)
        return s, bias

    def attend(i, s, bias):
        parts = []
        tops = []
        for j in range(i + 1):
            sj = s[j * blk:(j + 1) * blk, :]
            if j == i:
                sj = jnp.where(causal, sj, NEG_INF)
            top = jnp.max(sj, axis=0, keepdims=True)
            if j < i and bias[j] is not None:
                top = top + bias[j]
            parts.append(sj)
            tops.append(top)
        m = functools.reduce(jnp.maximum, tops)
        probs = []
        for j in range(i + 1):
            shift = m - bias[j] if (j < i and bias[j] is not None) else m
            probs.append(jnp.exp2(parts[j] - shift).astype(BF16))
        return jnp.concatenate(probs, axis=0) if i > 0 else probs[0]

    outs = []

    def values(i, h, pt):
        o = jnp.dot(v_ext[h][:, 0:(i + 1) * blk], pt, preferred_element_type=F32)
        outs.append(o[:hd] / o[hd:hd + 1])
        if h % HEAD_PAIR == HEAD_PAIR - 1:
            ot = jnp.concatenate(outs, axis=0)
            o_ref[0, i * blk:(i + 1) * blk, pair_lanes(h)] = ot.T.astype(BF16)
            outs.clear()

    units = [(i, h) for i in reversed(range(n_blk)) for h in range(n_heads)]
    pending = [scores(*u) for u in units[:SCORE_LOOKAHEAD]]
    probs_done = []
    for n, (i, h) in enumerate(units):
        cur = pending.pop(0)
        if n + SCORE_LOOKAHEAD < len(units):
            pending.append(scores(*units[n + SCORE_LOOKAHEAD]))
        probs_done.append((i, h, attend(i, *cur)))
        if len(probs_done) > VALUE_LAG:
            values(*probs_done.pop(0))
    for item in probs_done:
        values(*item)


def _moba_flash_kernel(qt_ref, k_ref, vt_ref, o_ref, *, n_blk, n_sel):
    blk = MOBA_BLOCK
    lanes = LANES
    hd = lanes // HEAD_PAIR
    n_heads = qt_ref.shape[1] // hd
    pair_lanes = lambda h: slice((h // HEAD_PAIR) * lanes, (h // HEAD_PAIR + 1) * lanes)
    gate_rows = 16
    means = [jnp.mean(k_ref[0, j * blk:(j + 1) * blk, :].astype(F32), axis=0, keepdims=True)
             for j in range(n_blk)]
    means.append(jnp.zeros((gate_rows - n_blk, k_ref.shape[2]), F32))
    kmean = jnp.concatenate(means, axis=0).astype(BF16)
    head_row = lax.broadcasted_iota(jnp.int32, (lanes, blk), 0) // hd
    blk_row = lax.broadcasted_iota(jnp.int32, (gate_rows, blk), 0)
    causal = (lax.broadcasted_iota(jnp.int32, (blk, blk), 0)
              <= lax.broadcasted_iota(jnp.int32, (blk, blk), 1))
    ones = jnp.ones((SUM_ROWS, blk), BF16)
    mm = lambda a, w: jnp.dot(a, w, preferred_element_type=F32)

    state = {}

    def open_unit(i, h):
        qp = qt_ref[0, pair_lanes(h), i * blk:(i + 1) * blk]
        qh = jnp.where(head_row == h % HEAD_PAIR, qp, jnp.zeros_like(qp))
        bias = [None] * i
        if i > n_sel:
            gate = mm(kmean[:, pair_lanes(h)], qh)
            for n in range(i):
                gn = gate[n:n + 1, :]
                ahead = (gate > gn) | ((gate == gn) & (blk_row < n))
                ahead = ahead & (blk_row < i)
                rank = jnp.sum(ahead.astype(F32), axis=0, keepdims=True)
                bias[n] = jnp.where(rank < n_sel, 0.0, NEG_INF)
        state[(i, h)] = [qh, bias, None, None]

    def score(i, h, j):
        return mm(k_ref[0, j * blk:(j + 1) * blk, pair_lanes(h)], state[(i, h)][0])

    def absorb(i, h, j, s):
        qh, bias, m_old, acc = state[(i, h)]
        bj = bias[j] if j < i else None
        if j == i:
            s = jnp.where(causal, s, NEG_INF)
        top = jnp.max(s, axis=0, keepdims=True)
        if bj is not None:
            top = top + bj
        m = top if m_old is None else jnp.maximum(m_old, top)
        p = jnp.exp2(s - (m if bj is None else m - bj)).astype(BF16)
        v = jnp.concatenate([vt_ref[0, h * hd:(h + 1) * hd, j * blk:(j + 1) * blk], ones], axis=0)
        o = mm(v, p)
        acc = o if acc is None else acc * jnp.exp2(m_old - m) + o
        state[(i, h)] = [qh, bias, m, acc]

    def close(i, outs):
        ot = jnp.concatenate([a[:hd] / a[hd:hd + 1] for a in outs], axis=0)
        return ot.T.astype(BF16)

    steps = [(i, h, j) for i in reversed(range(n_blk)) for j in reversed(range(i + 1))
             for h in range(n_heads)]
    opened = set()

    def issue(step):
        i, h, j = step
        if (i, h) not in opened:
            opened.add((i, h))
            open_unit(i, h)
        return score(i, h, j)

    pending = [issue(s) for s in steps[:FLASH_LOOKAHEAD]]
    for n, (i, h, j) in enumerate(steps):
        s = pending.pop(0)
        if n + FLASH_LOOKAHEAD < len(steps):
            pending.append(issue(steps[n + FLASH_LOOKAHEAD]))
        absorb(i, h, j, s)
        if j == 0 and h % HEAD_PAIR == HEAD_PAIR - 1:
            pair = [state.pop((i, hh))[3] for hh in range(h - HEAD_PAIR + 1, h + 1)]
            o_ref[0, i * blk:(i + 1) * blk, pair_lanes(h)] = close(i, pair)


def _moba(qt, k, vt):
    b, a, s = qt.shape
    lanes = MOBA_STEP_HEADS * (a // ATTN_HEADS)
    n_blk = s // MOBA_BLOCK
    tr = pl.BlockSpec((1, lanes, s), lambda bi, hi: (bi, hi, 0))
    tok = pl.BlockSpec((1, s, lanes), lambda bi, hi: (bi, 0, hi))
    return pl.pallas_call(
        functools.partial(_moba_flash_kernel, n_blk=n_blk, n_sel=min(MOBA_TOPK, n_blk - 1)),
        grid=(b, a // lanes),
        in_specs=[tr, tok, tr],
        out_specs=tok,
        out_shape=jax.ShapeDtypeStruct((b, s, a), BF16),
        compiler_params=_params(("parallel", "parallel")),
        name="moba",
    )(qt, k, vt)


def _post_kernel(yg_ref, at_ref, ga_ref, gb_ref, x_ref, p_ref, wglu_ref, bglu_ref, wa_ref, wb_ref,
                 wo_ref, gmix_ref, gpre_ref, w1_ref, w2_ref, gpost_ref, wple_ref, wpg_ref, gple_ref,
                 o_ref):
    tm = x_ref.shape[1]
    sub = min(ROW_SUB, tm)
    dff = w1_ref.shape[1]
    mm = lambda a, w: jnp.dot(a, w, preferred_element_type=F32)
    tiles = [slice(r, r + sub) for r in range(0, tm, sub)]
    chunks = [slice(rs.start // SSM_CHUNK, rs.stop // SSM_CHUNK) for rs in tiles]
    yg = [jnp.concatenate([yg_ref[o, cs].reshape(sub, LANES) for o in range(yg_ref.shape[0])], axis=1)
          for cs in chunks]
    glu = [mm(v.astype(BF16), wglu_ref[...]) + bglu_ref[...] for v in yg]
    yb = [mm(at_ref[0, rs], wb_ref[...]) for rs in tiles]
    ya_in = [(v * jax.nn.sigmoid(s)).astype(BF16) for v, s in zip(yg, glu)]
    ya = [mm(v, wa_ref[...]) for v in ya_in]
    mixed = [(ga_ref[0, rs].astype(F32) * a + gb_ref[0, rs].astype(F32) * b).astype(BF16)
             for rs, a, b in zip(tiles, ya, yb)]
    mo = [mm(v, wo_ref[...]) for v in mixed]
    e = [mm(p_ref[0, rs].astype(BF16), wple_ref[...]) for rs in tiles]
    xs = [x_ref[0, rs] + _rms(v) * gmix_ref[...] for rs, v in zip(tiles, mo)]

    hm = [(_rms(x) * gpre_ref[...]).astype(BF16) for x in xs]
    f = [None] * len(tiles)
    for c in range(0, dff, FF_CHUNK):
        hid = [mm(h, w1_ref[:, c:c + FF_CHUNK]) for h in hm]
        hid = [jnp.square(jnp.maximum(h, 0.0)).astype(BF16) for h in hid]
        part = [mm(h, w2_ref[c:c + FF_CHUNK, :]) for h in hid]
        f = [v if acc is None else acc + v for acc, v in zip(f, part)]
    xs = [x + _rms(v) * gpost_ref[...] for x, v in zip(xs, f)]

    gate = [mm(x.astype(BF16), wpg_ref[...]) for x in xs]
    for rs, x, ev, gv in zip(tiles, xs, e, gate):
        o_ref[0, rs] = x + _rms(ev * jax.nn.sigmoid(gv)) * gple_ref[...]


def _post(yg5, at, ga, gb, x, p, consts):
    b, s, d = x.shape
    tm = min(POST_TILE, s)
    tok = lambda w: pl.BlockSpec((1, tm, w), lambda bi, ti: (bi, ti, 0))
    return pl.pallas_call(
        _post_kernel,
        grid=(b, s // tm),
        in_specs=[_chunk_tile_spec(yg5.shape[0], tm), tok(at.shape[2]), tok(d), tok(d), tok(d),
                  tok(p.shape[2])] + [_const_spec(c.shape) for c in consts],
        out_specs=tok(d),
        out_shape=jax.ShapeDtypeStruct((b, s, d), F32),
        compiler_params=_params(("parallel", "parallel")),
        name="post",
    )(yg5, at, ga, gb, x, p, *consts)


def kernel(x, p, g_pre_mix, w_in, ssm_lam_re, ssm_lam_im, ssm_log_dt, ssm_b_re, ssm_b_im, ssm_c_re,
           ssm_c_im, ssm_d, w_glu, b_glu, w_branch_a, w_branch_b, w_out, g_post_mix, g_pre_mlp,
           w_mlp1, w_mlp2, g_post_mlp, w_ple, w_ple_gate, g_ple):
    b, s, d = x.shape
    depth = w_in.shape[0]
    wdt = w_glu.shape[1]
    a = w_branch_b.shape[1]
    hd = a // ATTN_HEADS
    assert hd * HEAD_PAIR == LANES and ATTN_HEADS % MOBA_STEP_HEADS == 0
    assert wdt % LANES == 0 and s % MOBA_BLOCK == 0 and s % min(TOKEN_TILE, s) == 0
    assert (s // SSM_CHUNK) % min(SCAN_CHUNKS, s // SSM_CHUNK) == 0
    row = lambda v: v.reshape(1, -1)
    for i in range(depth):
        w = w_in[i].astype(BF16)
        o_q, o_k, o_v, o_g = wdt, wdt + a, wdt + 2 * a, wdt + 3 * a
        u5, qt, k, vt, ga, gb = _in_proj(
            x, row(g_pre_mix[i]), w[:, :o_q], w[:, o_q:o_k].T, w[:, o_k:o_v], w[:, o_v:o_g].T,
            w[:, o_g:], float(hd) ** -0.5 * LOG2E)

        prep = _ssm_prep(ssm_lam_re[i], ssm_lam_im[i], ssm_log_dt[i], ssm_b_re[i], ssm_b_im[i],
                         ssm_c_re[i], ssm_c_im[i])
        dvec = jnp.tile(ssm_d[i].reshape(-1, 1, LANES), (1, 1, SSM_CHUNK))
        yg5 = _s5(u5, *prep, dvec)

        at = _moba(qt, k, vt)

        x = _post(yg5, at, ga, gb, x, p[i],
                  [w_glu[i].astype(BF16), row(b_glu[i]), w_branch_a[i].astype(BF16),
                   w_branch_b[i].astype(BF16), w_out[i].astype(BF16), row(g_post_mix[i]),
                   row(g_pre_mlp[i]), w_mlp1[i].astype(BF16), w_mlp2[i].astype(BF16),
                   row(g_post_mlp[i]), w_ple[i].astype(BF16), w_ple_gate[i].astype(BF16),
                   row(g_ple[i])])
    return x
```

```python
import functools

import jax
import jax.numpy as jnp
from jax import lax
from jax.experimental import pallas as pl
from jax.experimental.pallas import tpu as pltpu

F32 = jnp.float32
BF16 = jnp.bfloat16

SSM_GROUP = 16
SSM_STATE = 64
ATTN_HEADS = 8
MOBA_BLOCK = 256
MOBA_TOPK = 3
RMS_EPS = 1e-6
NEG_INF = -1e30
LOG2E = 1.4426950408889634

LANES = 128
F32_SUBLANES = 8
VMEM_LIMIT = 48 * 1024 * 1024

SSM_CHUNK = F32_SUBLANES
TILE_GROUPS = LANES // SSM_GROUP
SCAN_CHUNKS = 32
S5_SUB_BLOCKS = 4
HEAD_PAIR = 2
MOBA_STEP_HEADS = 4
SUM_ROWS = 16
FLASH_CHAINS = 4
FLASH_LOOKAHEAD = 5
TOKEN_TILE = 1024
POST_TILE = 512
ROW_SUB = 256
FF_CHUNK = 1024

_NT = (((1,), (1,)), ((), ()))


def _rms(v):
    return v * lax.rsqrt(jnp.mean(v * v, axis=-1, keepdims=True) + RMS_EPS)


def _const_spec(shape):
    nd = len(shape)
    return pl.BlockSpec(shape, lambda *_: (0,) * nd, pipeline_mode=pl.Buffered(1))


def _params(sem):
    return pltpu.CompilerParams(dimension_semantics=sem, vmem_limit_bytes=VMEM_LIMIT)


def _chunk_tile_spec(n_tiles, tm):
    return pl.BlockSpec((n_tiles, tm // SSM_CHUNK, None, SSM_CHUNK, LANES),
                        lambda bi, ti: (0, ti, bi, 0, 0))


def _in_proj_kernel(x_ref, g_ref, wu_ref, wqt_ref, wk_ref, wvt_ref, wg_ref,
                    u_ref, qt_ref, k_ref, vt_ref, ga_ref, gb_ref, *, q_scale):
    tm, d = x_ref.shape[1], x_ref.shape[2]
    sub = min(ROW_SUB, tm)
    tiles = [slice(r, r + sub) for r in range(0, tm, sub)]
    norm = lambda rs: (_rms(x_ref[0, rs]) * g_ref[...]).astype(BF16)
    hbs = [norm(tiles[0])]
    for n, rs in enumerate(tiles):
        hb = hbs[n]
        if n + 1 < len(tiles):
            hbs.append(norm(tiles[n + 1]))
        cs = slice(rs.start // SSM_CHUNK, rs.stop // SSM_CHUNK)
        u = jnp.dot(hb, wu_ref[...], preferred_element_type=F32)
        for o in range(u_ref.shape[0]):
            u_ref[o, cs] = u[:, o * LANES:(o + 1) * LANES].reshape(sub // SSM_CHUNK, SSM_CHUNK, LANES)
        k_ref[0, rs] = jnp.dot(hb, wk_ref[...], preferred_element_type=F32).astype(BF16)
        qt = lax.dot_general(wqt_ref[...], hb, _NT, preferred_element_type=F32)
        qt_ref[0, :, rs] = (qt * q_scale).astype(BF16)
        vt_ref[0, :, rs] = lax.dot_general(wvt_ref[...], hb, _NT,
                                           preferred_element_type=F32).astype(BF16)
        gates = jax.nn.sigmoid(jnp.dot(hb, wg_ref[...], preferred_element_type=F32))
        ga_ref[0, rs] = gates[:, :d].astype(BF16)
        gb_ref[0, rs] = gates[:, d:].astype(BF16)


def _in_proj(x, g, wu, wqt, wk, wvt, wg, q_scale):
    b, s, d = x.shape
    n_tiles = wu.shape[1] // LANES
    a = wk.shape[1]
    tm = min(TOKEN_TILE, s)
    tok = lambda w: pl.BlockSpec((1, tm, w), lambda bi, ti: (bi, ti, 0))
    tr = lambda w: pl.BlockSpec((1, w, tm), lambda bi, ti: (bi, 0, ti))
    return pl.pallas_call(
        functools.partial(_in_proj_kernel, q_scale=q_scale),
        grid=(b, s // tm),
        in_specs=[tok(d), _const_spec(g.shape), _const_spec(wu.shape), _const_spec(wqt.shape),
                  _const_spec(wk.shape), _const_spec(wvt.shape), _const_spec(wg.shape)],
        out_specs=[_chunk_tile_spec(n_tiles, tm), tr(a), tok(a), tr(a), tok(d), tok(d)],
        out_shape=[jax.ShapeDtypeStruct((n_tiles, s // SSM_CHUNK, b, SSM_CHUNK, LANES), F32),
                   jax.ShapeDtypeStruct((b, a, s), BF16), jax.ShapeDtypeStruct((b, s, a), BF16),
                   jax.ShapeDtypeStruct((b, a, s), BF16),
                   jax.ShapeDtypeStruct((b, s, d), BF16), jax.ShapeDtypeStruct((b, s, d), BF16)],
        compiler_params=_params(("parallel", "parallel")),
        name="in_proj",
    )(x, g, wu, wqt, wk, wvt, wg)


def _ssm_prep_kernel(lre_ref, lim_ref, ldt_ref, btr_ref, bti_ref, cr_ref, ci_ref,
                     toep_ref, wst_ref, wout_ref, ltr_ref, lti_ref):
    t = SSM_CHUNK
    rows, width = cr_ref.shape[1], cr_ref.shape[2]
    lre = lre_ref[0]
    lim = lim_ref[0]
    dt = jnp.exp(ldt_ref[0])
    ea = lre * dt
    eb = lim * dt

    def power(tau):
        mag = jnp.exp(tau * ea)
        return mag * jnp.cos(tau * eb), mag * jnp.sin(tau * eb)

    def times(pw, mr, mi):
        return pw[0] * mr - pw[1] * mi, pw[0] * mi + pw[1] * mr

    pw = [power(float(tau)) for tau in range(t + 1)]
    ltr_ref[0], lti_ref[0] = pw[t]
    same_group = (lax.broadcasted_iota(jnp.int32, (rows, width), 0) // SSM_GROUP
                  == lax.broadcasted_iota(jnp.int32, (rows, width), 1) // SSM_STATE)
    keep = lambda ref: jnp.where(same_group, ref[0], 0.0)
    cr, ci = keep(cr_ref), keep(ci_ref)
    nr = pw[1][0] - 1.0
    ni = pw[1][1]
    den = lre * lre + lim * lim
    coef = ((nr * lre + ni * lim) / den, (ni * lre - nr * lim) / den)
    bbr, bbi = times(coef, keep(btr_ref), keep(bti_ref))

    for s in range(t):
        wr, wi = times(pw[t - 1 - s], bbr, bbi)
        wst_ref[0, s * rows:(s + 1) * rows, :] = jnp.concatenate([wr, wi], axis=1).astype(BF16)
        orr, oi = times(pw[s + 1], cr, ci)
        wout_ref[0, :width, s * rows:(s + 1) * rows] = orr.T.astype(BF16)
        wout_ref[0, width:, s * rows:(s + 1) * rows] = (-oi).T.astype(BF16)

    cl = [times(pw[tau], cr, ci) for tau in range(t)]
    clr = jnp.concatenate([m[0] for m in cl], axis=0)
    cli = jnp.concatenate([m[1] for m in cl], axis=0)
    hi = lax.Precision.HIGHEST
    bd = (lax.dot_general(bbr, clr, _NT, precision=hi, preferred_element_type=F32)
          - lax.dot_general(bbi, cli, _NT, precision=hi, preferred_element_type=F32)).astype(BF16)
    for s in range(t):
        lead = [jnp.zeros((rows, s * rows), BF16)] if s else []
        toep_ref[0, s * rows:(s + 1) * rows, :] = jnp.concatenate(
            lead + [bd[:, :(t - s) * rows]], axis=1)


def _ssm_prep(lam_re, lam_im, log_dt, b_re, b_im, c_re, c_im):
    g, p = lam_re.shape
    tg, t = TILE_GROUPS, SSM_CHUNK
    n = g // tg
    width = tg * p
    lanes = lambda v: v.reshape(n, 1, width)
    chan = lambda m: jnp.tile(m.reshape(n, LANES, p), (1, 1, tg))
    vec = pl.BlockSpec((1, 1, width), lambda ti: (ti, 0, 0))
    mat = pl.BlockSpec((1, LANES, width), lambda ti: (ti, 0, 0))
    sq = lambda r, c: pl.BlockSpec((1, r, c), lambda ti: (ti, 0, 0))
    return pl.pallas_call(
        _ssm_prep_kernel,
        grid=(n,),
        in_specs=[vec, vec, vec, mat, mat, mat, mat],
        out_specs=[sq(t * LANES, t * LANES), sq(t * LANES, 2 * width), sq(2 * width, t * LANES),
                   vec, vec],
        out_shape=[jax.ShapeDtypeStruct((n, t * LANES, t * LANES), BF16),
                   jax.ShapeDtypeStruct((n, t * LANES, 2 * width), BF16),
                   jax.ShapeDtypeStruct((n, 2 * width, t * LANES), BF16),
                   jax.ShapeDtypeStruct((n, 1, width), F32), jax.ShapeDtypeStruct((n, 1, width), F32)],
        compiler_params=_params(("parallel",)),
        name="ssm_prep",
    )(lanes(lam_re), lanes(lam_im), lanes(jnp.repeat(log_dt, p)),
      chan(jnp.swapaxes(b_re, 1, 2)), chan(jnp.swapaxes(b_im, 1, 2)), chan(c_re), chan(c_im))


def _s5_kernel(u_ref, m_ref, wst_ref, wout_ref, lr_ref, li_ref, d_ref, y_ref, h_ref, st_ref, *, nb):
    t = SSM_CHUNK
    rows = u_ref.shape[0] // t
    half = lr_ref.shape[-1]

    @pl.when(pl.program_id(1) == 0)
    def _():
        st_ref[...] = jnp.zeros(st_ref.shape, F32)

    rp = rows // S5_SUB_BLOCKS
    parts = [slice(p * rp, (p + 1) * rp) for p in range(S5_SUB_BLOCKS)]
    u32 = [jnp.concatenate([u_ref[pl.ds(rs.start * t + s, rp, stride=t), :] for s in range(t)], axis=1)
           for rs in parts]
    u = [v.astype(BF16) for v in u32]
    for rs, v in zip(parts, u):
        h_ref[rs, :] = jnp.dot(v, wst_ref[...], preferred_element_type=F32)
    lam_r = lr_ref[...]
    lam_i = li_ref[...]

    def step(k, carry):
        sr, si = carry
        rk = pl.ds(pl.multiple_of(k * nb, nb), nb)
        hr = h_ref[rk, :half]
        hi = h_ref[rk, half:]
        h_ref[rk, :half] = sr
        h_ref[rk, half:] = si
        return (lam_r * sr - lam_i * si + hr, lam_r * si + lam_i * sr + hi)

    y = [jnp.dot(v, m_ref[...], preferred_element_type=F32) + d_ref[...] * v32
         for v, v32 in zip(u, u32)]
    sr, si = lax.fori_loop(0, rows // nb, step, (st_ref[0], st_ref[1]), unroll=True)
    st_ref[0] = sr
    st_ref[1] = si
    y = [v + jnp.dot(h_ref[rs, :].astype(BF16), wout_ref[...], preferred_element_type=F32)
         for rs, v in zip(parts, y)]
    for rs, v in zip(parts, y):
        v = jax.nn.gelu(v)
        for s in range(t):
            y_ref[pl.ds(rs.start * t + s, rp, stride=t), :] = v[:, s * LANES:(s + 1) * LANES]


def _s5(u5, toep, wst, wout, lam_r, lam_i, dvec):
    n_tiles, n_chunks, nb, t, lanes = u5.shape
    kb = min(SCAN_CHUNKS, n_chunks)
    blk_tokens = kb * nb * t
    flat = u5.reshape(n_tiles, n_chunks * nb * t, lanes)
    width = toep.shape[-1]
    st = wst.shape[-1]
    tok = pl.BlockSpec((None, blk_tokens, lanes), lambda oi, ki: (oi, ki, 0))
    op = lambda r, c: pl.BlockSpec((None, r, c), lambda oi, ki: (oi, 0, 0))
    y = pl.pallas_call(
        functools.partial(_s5_kernel, nb=nb),
        grid=(n_tiles, n_chunks // kb),
        in_specs=[tok, op(width, width), op(width, st), op(st, width), op(1, st // 2), op(1, st // 2),
                  op(1, width)],
        out_specs=tok,
        out_shape=jax.ShapeDtypeStruct(flat.shape, F32),
        scratch_shapes=[pltpu.VMEM((kb * nb, st), F32), pltpu.VMEM((2, nb, st // 2), F32)],
        compiler_params=_params(("parallel", "arbitrary")),
        name="s5",
    )(flat, toep, wst, wout, lam_r, lam_i, dvec)
    return y.reshape(u5.shape)


def _moba_kernel(qt_ref, k_ref, vt_ref, o_ref, *, n_blk, n_sel):
    blk = MOBA_BLOCK
    lanes = LANES
    hd = lanes // HEAD_PAIR
    n_heads = qt_ref.shape[1] // hd
    pair_lanes = lambda h: slice((h // HEAD_PAIR) * lanes, (h // HEAD_PAIR + 1) * lanes)
    gate_rows = 16
    means = [jnp.mean(k_ref[0, j * blk:(j + 1) * blk, :].astype(F32), axis=0, keepdims=True)
             for j in range(n_blk)]
    means.append(jnp.zeros((gate_rows - n_blk, k_ref.shape[2]), F32))
    kmean = jnp.concatenate(means, axis=0).astype(BF16)
    head_row = lax.broadcasted_iota(jnp.int32, (lanes, blk), 0) // hd
    blk_row = lax.broadcasted_iota(jnp.int32, (gate_rows, blk), 0)
    causal = (lax.broadcasted_iota(jnp.int32, (blk, blk), 0)
              <= lax.broadcasted_iota(jnp.int32, (blk, blk), 1))
    ones = jnp.ones((SUM_ROWS, blk), BF16)
    mm = lambda a, w: jnp.dot(a, w, preferred_element_type=F32)

    state = {}

    def open_unit(i, h):
        qp = qt_ref[0, pair_lanes(h), i * blk:(i + 1) * blk]
        qh = jnp.where(head_row == h % HEAD_PAIR, qp, jnp.zeros_like(qp))
        bias = [None] * i
        if i > n_sel:
            gate = mm(kmean[:, pair_lanes(h)], qh)
            for n in range(i):
                gn = gate[n:n + 1, :]
                ahead = (gate > gn) | ((gate == gn) & (blk_row < n))
                ahead = ahead & (blk_row < i)
                rank = jnp.sum(ahead.astype(F32), axis=0, keepdims=True)
                bias[n] = jnp.where(rank < n_sel, 0.0, NEG_INF)
        state[(i, h)] = [qh, bias, None, None]

    def score(i, h, j):
        if (i, h) not in state:
            open_unit(i, h)
        return mm(k_ref[0, j * blk:(j + 1) * blk, pair_lanes(h)], state[(i, h)][0])

    def absorb(i, h, j, s):
        qh, bias, m_old, acc = state[(i, h)]
        bj = bias[j] if j < i else None
        if j == i:
            s = jnp.where(causal, s, NEG_INF)
        top = jnp.max(s, axis=0, keepdims=True)
        if bj is not None:
            top = top + bj
        m = top if m_old is None else jnp.maximum(m_old, top)
        p = jnp.exp2(s - (m if bj is None else m - bj)).astype(BF16)
        v = jnp.concatenate([vt_ref[0, h * hd:(h + 1) * hd, j * blk:(j + 1) * blk], ones], axis=0)
        o = mm(v, p)
        acc = o if acc is None else acc * jnp.exp2(m_old - m) + o
        state[(i, h)] = [qh, bias, m, acc]

    units = [(i, h) for i in reversed(range(n_blk)) for h in range(n_heads)]
    steps = []
    for w in range(0, len(units), FLASH_CHAINS):
        wave = units[w:w + FLASH_CHAINS]
        for t in range(max(i for i, _ in wave) + 1):
            steps += [(i, h, i - t) for i, h in wave if t <= i]
    pending = [score(*s) for s in steps[:FLASH_LOOKAHEAD]]
    for n, (i, h, j) in enumerate(steps):
        s = pending.pop(0)
        if n + FLASH_LOOKAHEAD < len(steps):
            pending.append(score(*steps[n + FLASH_LOOKAHEAD]))
        absorb(i, h, j, s)
        if j == 0 and h % HEAD_PAIR == HEAD_PAIR - 1:
            accs = [state.pop((i, hh))[3] for hh in range(h - HEAD_PAIR + 1, h + 1)]
            ot = jnp.concatenate([a[:hd] / a[hd:hd + 1] for a in accs], axis=0)
            o_ref[0, i * blk:(i + 1) * blk, pair_lanes(h)] = ot.T.astype(BF16)


def _moba(qt, k, vt):
    b, a, s = qt.shape
    lanes = MOBA_STEP_HEADS * (a // ATTN_HEADS)
    n_blk = s // MOBA_BLOCK
    tr = pl.BlockSpec((1, lanes, s), lambda bi, hi: (bi, hi, 0))
    tok = pl.BlockSpec((1, s, lanes), lambda bi, hi: (bi, 0, hi))
    return pl.pallas_call(
        functools.partial(_moba_kernel, n_blk=n_blk, n_sel=min(MOBA_TOPK, n_blk - 1)),
        grid=(b, a // lanes),
        in_specs=[tr, tok, tr],
        out_specs=tok,
        out_shape=jax.ShapeDtypeStruct((b, s, a), BF16),
        compiler_params=_params(("parallel", "parallel")),
        name="moba",
    )(qt, k, vt)


def _post_kernel(yg_ref, at_ref, ga_ref, gb_ref, x_ref, p_ref, wglu_ref, bglu_ref, wa_ref, wb_ref,
                 wo_ref, gmix_ref, gpre_ref, w1_ref, w2_ref, gpost_ref, wple_ref, wpg_ref, gple_ref,
                 o_ref):
    tm = x_ref.shape[1]
    sub = min(ROW_SUB, tm)
    dff = w1_ref.shape[1]
    mm = lambda a, w: jnp.dot(a, w, preferred_element_type=F32)
    tiles = [slice(r, r + sub) for r in range(0, tm, sub)]
    chunks = [slice(rs.start // SSM_CHUNK, rs.stop // SSM_CHUNK) for rs in tiles]
    yg = [jnp.concatenate([yg_ref[o, cs].reshape(sub, LANES) for o in range(yg_ref.shape[0])], axis=1)
          for cs in chunks]
    glu = [mm(v.astype(BF16), wglu_ref[...]) + bglu_ref[...] for v in yg]
    yb = [mm(at_ref[0, rs], wb_ref[...]) for rs in tiles]
    ya_in = [(v * jax.nn.sigmoid(s)).astype(BF16) for v, s in zip(yg, glu)]
    ya = [mm(v, wa_ref[...]) for v in ya_in]
    mixed = [(ga_ref[0, rs].astype(F32) * a + gb_ref[0, rs].astype(F32) * b).astype(BF16)
             for rs, a, b in zip(tiles, ya, yb)]
    mo = [mm(v, wo_ref[...]) for v in mixed]
    e = [mm(p_ref[0, rs].astype(BF16), wple_ref[...]) for rs in tiles]
    xs = [x_ref[0, rs] + _rms(v) * gmix_ref[...] for rs, v in zip(tiles, mo)]

    hm = [(_rms(x) * gpre_ref[...]).astype(BF16) for x in xs]
    f = [None] * len(tiles)
    for c in range(0, dff, FF_CHUNK):
        hid = [mm(h, w1_ref[:, c:c + FF_CHUNK]) for h in hm]
        hid = [jnp.square(jnp.maximum(h, 0.0)).astype(BF16) for h in hid]
        part = [mm(h, w2_ref[c:c + FF_CHUNK, :]) for h in hid]
        f = [v if acc is None else acc + v for acc, v in zip(f, part)]
    xs = [x + _rms(v) * gpost_ref[...] for x, v in zip(xs, f)]

    gate = [mm(x.astype(BF16), wpg_ref[...]) for x in xs]
    for rs, x, ev, gv in zip(tiles, xs, e, gate):
        o_ref[0, rs] = x + _rms(ev * jax.nn.sigmoid(gv)) * gple_ref[...]


def _post(yg5, at, ga, gb, x, p, consts):
    b, s, d = x.shape
    tm = min(POST_TILE, s)
    tok = lambda w: pl.BlockSpec((1, tm, w), lambda bi, ti: (bi, ti, 0))
    return pl.pallas_call(
        _post_kernel,
        grid=(b, s // tm),
        in_specs=[_chunk_tile_spec(yg5.shape[0], tm), tok(at.shape[2]), tok(d), tok(d), tok(d),
                  tok(p.shape[2])] + [_const_spec(c.shape) for c in consts],
        out_specs=tok(d),
        out_shape=jax.ShapeDtypeStruct((b, s, d), F32),
        compiler_params=_params(("parallel", "parallel")),
        name="post",
    )(yg5, at, ga, gb, x, p, *consts)


def kernel(x, p, g_pre_mix, w_in, ssm_lam_re, ssm_lam_im, ssm_log_dt, ssm_b_re, ssm_b_im, ssm_c_re,
           ssm_c_im, ssm_d, w_glu, b_glu, w_branch_a, w_branch_b, w_out, g_post_mix, g_pre_mlp,
           w_mlp1, w_mlp2, g_post_mlp, w_ple, w_ple_gate, g_ple):
    b, s, d = x.shape
    depth = w_in.shape[0]
    wdt = w_glu.shape[1]
    a = w_branch_b.shape[1]
    hd = a // ATTN_HEADS
    assert hd * HEAD_PAIR == LANES and ATTN_HEADS % MOBA_STEP_HEADS == 0
    assert wdt % LANES == 0 and s % MOBA_BLOCK == 0 and s % min(TOKEN_TILE, s) == 0
    assert (s // SSM_CHUNK) % min(SCAN_CHUNKS, s // SSM_CHUNK) == 0
    row = lambda v: v.reshape(1, -1)
    for i in range(depth):
        w = w_in[i].astype(BF16)
        o_q, o_k, o_v, o_g = wdt, wdt + a, wdt + 2 * a, wdt + 3 * a
        u5, qt, k, vt, ga, gb = _in_proj(
            x, row(g_pre_mix[i]), w[:, :o_q], w[:, o_q:o_k].T, w[:, o_k:o_v], w[:, o_v:o_g].T,
            w[:, o_g:], float(hd) ** -0.5 * LOG2E)

        prep = _ssm_prep(ssm_lam_re[i], ssm_lam_im[i], ssm_log_dt[i], ssm_b_re[i], ssm_b_im[i],
                         ssm_c_re[i], ssm_c_im[i])
        dvec = jnp.tile(ssm_d[i].reshape(-1, 1, LANES), (1, 1, SSM_CHUNK))
        yg5 = _s5(u5, *prep, dvec)

        at = _moba(qt, k, vt)

        x = _post(yg5, at, ga, gb, x, p[i],
                  [w_glu[i].astype(BF16), row(b_glu[i]), w_branch_a[i].astype(BF16),
                   w_branch_b[i].astype(BF16), w_out[i].astype(BF16), row(g_post_mix[i]),
                   row(g_pre_mlp[i]), w_mlp1[i].astype(BF16), w_mlp2[i].astype(BF16),
                   row(g_post_mlp[i]), w_ple[i].astype(BF16), w_ple_gate[i].astype(BF16),
                   row(g_ple[i])])
    return x
```

```python
import functools

import jax
import jax.numpy as jnp
from jax import lax
from jax.experimental import pallas as pl
from jax.experimental.pallas import tpu as pltpu

F32 = jnp.float32
BF16 = jnp.bfloat16

SSM_GROUP = 16
SSM_STATE = 64
ATTN_HEADS = 8
MOBA_BLOCK = 256
MOBA_TOPK = 3
RMS_EPS = 1e-6
NEG_INF = -1e30
LOG2E = 1.4426950408889634

LANES = 128
F32_SUBLANES = 8
VMEM_LIMIT = 48 * 1024 * 1024

SSM_CHUNK = F32_SUBLANES
TILE_GROUPS = LANES // SSM_GROUP
SCAN_CHUNKS = 32
S5_SUB_BLOCKS = 4
HEAD_PAIR = 2
MOBA_STEP_HEADS = 4
SUM_ROWS = 16
FLASH_CHAINS = 4
FLASH_LOOKAHEAD = 5
TOKEN_TILE = 1024
POST_TILE = 512
ROW_SUB = 256
FF_CHUNK = 1024

_NT = (((1,), (1,)), ((), ()))


def _rms(v):
    return v * lax.rsqrt(jnp.mean(v * v, axis=-1, keepdims=True) + RMS_EPS)


def _const_spec(shape):
    nd = len(shape)
    return pl.BlockSpec(shape, lambda *_: (0,) * nd, pipeline_mode=pl.Buffered(1))


def _params(sem):
    return pltpu.CompilerParams(dimension_semantics=sem, vmem_limit_bytes=VMEM_LIMIT)


def _chunk_tile_spec(n_tiles, tm):
    return pl.BlockSpec((n_tiles, tm // SSM_CHUNK, None, SSM_CHUNK, LANES),
                        lambda bi, ti: (0, ti, bi, 0, 0))


def _in_proj_kernel(x_ref, g_ref, wu_ref, wqt_ref, wk_ref, wvt_ref, wg_ref,
                    u_ref, qt_ref, k_ref, vt_ref, ga_ref, gb_ref, *, q_scale):
    tm, d = x_ref.shape[1], x_ref.shape[2]
    sub = min(ROW_SUB, tm)
    tiles = [slice(r, r + sub) for r in range(0, tm, sub)]
    norm = lambda rs: (_rms(x_ref[0, rs]) * g_ref[...]).astype(BF16)
    hbs = [norm(tiles[0])]
    for n, rs in enumerate(tiles):
        hb = hbs[n]
        if n + 1 < len(tiles):
            hbs.append(norm(tiles[n + 1]))
        cs = slice(rs.start // SSM_CHUNK, rs.stop // SSM_CHUNK)
        gates = jax.nn.sigmoid(jnp.dot(hb, wg_ref[...], preferred_element_type=F32))
        ga_ref[0, rs] = gates[:, :d].astype(BF16)
        gb_ref[0, rs] = gates[:, d:].astype(BF16)
        u = jnp.dot(hb, wu_ref[...], preferred_element_type=F32)
        for o in range(u_ref.shape[0]):
            u_ref[o, cs] = u[:, o * LANES:(o + 1) * LANES].reshape(sub // SSM_CHUNK, SSM_CHUNK, LANES)
        k_ref[0, rs] = jnp.dot(hb, wk_ref[...], preferred_element_type=F32).astype(BF16)
        qt = lax.dot_general(wqt_ref[...], hb, _NT, preferred_element_type=F32)
        qt_ref[0, :, rs] = (qt * q_scale).astype(BF16)
        vt_ref[0, :, rs] = lax.dot_general(wvt_ref[...], hb, _NT,
                                           preferred_element_type=F32).astype(BF16)


def _in_proj(x, g, wu, wqt, wk, wvt, wg, q_scale):
    b, s, d = x.shape
    n_tiles = wu.shape[1] // LANES
    a = wk.shape[1]
    tm = min(TOKEN_TILE, s)
    tok = lambda w: pl.BlockSpec((1, tm, w), lambda bi, ti: (bi, ti, 0))
    tr = lambda w: pl.BlockSpec((1, w, tm), lambda bi, ti: (bi, 0, ti))
    return pl.pallas_call(
        functools.partial(_in_proj_kernel, q_scale=q_scale),
        grid=(b, s // tm),
        in_specs=[tok(d), _const_spec(g.shape), _const_spec(wu.shape), _const_spec(wqt.shape),
                  _const_spec(wk.shape), _const_spec(wvt.shape), _const_spec(wg.shape)],
        out_specs=[_chunk_tile_spec(n_tiles, tm), tr(a), tok(a), tr(a), tok(d), tok(d)],
        out_shape=[jax.ShapeDtypeStruct((n_tiles, s // SSM_CHUNK, b, SSM_CHUNK, LANES), F32),
                   jax.ShapeDtypeStruct((b, a, s), BF16), jax.ShapeDtypeStruct((b, s, a), BF16),
                   jax.ShapeDtypeStruct((b, a, s), BF16),
                   jax.ShapeDtypeStruct((b, s, d), BF16), jax.ShapeDtypeStruct((b, s, d), BF16)],
        compiler_params=_params(("parallel", "parallel")),
        name="in_proj",
    )(x, g, wu, wqt, wk, wvt, wg)


def _ssm_prep_kernel(lre_ref, lim_ref, ldt_ref, btr_ref, bti_ref, cr_ref, ci_ref,
                     toep_ref, wst_ref, wout_ref, ltr_ref, lti_ref):
    t = SSM_CHUNK
    rows, width = cr_ref.shape[1], cr_ref.shape[2]
    lre = lre_ref[0]
    lim = lim_ref[0]
    dt = jnp.exp(ldt_ref[0])
    ea = lre * dt
    eb = lim * dt

    def power(tau):
        mag = jnp.exp(tau * ea)
        return mag * jnp.cos(tau * eb), mag * jnp.sin(tau * eb)

    def times(pw, mr, mi):
        return pw[0] * mr - pw[1] * mi, pw[0] * mi + pw[1] * mr

    pw = [power(float(tau)) for tau in range(t + 1)]
    ltr_ref[0], lti_ref[0] = pw[t]
    same_group = (lax.broadcasted_iota(jnp.int32, (rows, width), 0) // SSM_GROUP
                  == lax.broadcasted_iota(jnp.int32, (rows, width), 1) // SSM_STATE)
    keep = lambda ref: jnp.where(same_group, ref[0], 0.0)
    cr, ci = keep(cr_ref), keep(ci_ref)
    nr = pw[1][0] - 1.0
    ni = pw[1][1]
    den = lre * lre + lim * lim
    coef = ((nr * lre + ni * lim) / den, (ni * lre - nr * lim) / den)
    bbr, bbi = times(coef, keep(btr_ref), keep(bti_ref))

    for s in range(t):
        wr, wi = times(pw[t - 1 - s], bbr, bbi)
        wst_ref[0, s * rows:(s + 1) * rows, :] = jnp.concatenate([wr, wi], axis=1).astype(BF16)
        orr, oi = times(pw[s + 1], cr, ci)
        wout_ref[0, :width, s * rows:(s + 1) * rows] = orr.T.astype(BF16)
        wout_ref[0, width:, s * rows:(s + 1) * rows] = (-oi).T.astype(BF16)

    cl = [times(pw[tau], cr, ci) for tau in range(t)]
    clr = jnp.concatenate([m[0] for m in cl], axis=0)
    cli = jnp.concatenate([m[1] for m in cl], axis=0)
    hi = lax.Precision.HIGHEST
    bd = (lax.dot_general(bbr, clr, _NT, precision=hi, preferred_element_type=F32)
          - lax.dot_general(bbi, cli, _NT, precision=hi, preferred_element_type=F32)).astype(BF16)
    for s in range(t):
        lead = [jnp.zeros((rows, s * rows), BF16)] if s else []
        toep_ref[0, s * rows:(s + 1) * rows, :] = jnp.concatenate(
            lead + [bd[:, :(t - s) * rows]], axis=1)


def _ssm_prep(lam_re, lam_im, log_dt, b_re, b_im, c_re, c_im):
    g, p = lam_re.shape
    tg, t = TILE_GROUPS, SSM_CHUNK
    n = g // tg
    width = tg * p
    lanes = lambda v: v.reshape(n, 1, width)
    chan = lambda m: jnp.tile(m.reshape(n, LANES, p), (1, 1, tg))
    vec = pl.BlockSpec((1, 1, width), lambda ti: (ti, 0, 0))
    mat = pl.BlockSpec((1, LANES, width), lambda ti: (ti, 0, 0))
    sq = lambda r, c: pl.BlockSpec((1, r, c), lambda ti: (ti, 0, 0))
    return pl.pallas_call(
        _ssm_prep_kernel,
        grid=(n,),
        in_specs=[vec, vec, vec, mat, mat, mat, mat],
        out_specs=[sq(t * LANES, t * LANES), sq(t * LANES, 2 * width), sq(2 * width, t * LANES),
                   vec, vec],
        out_shape=[jax.ShapeDtypeStruct((n, t * LANES, t * LANES), BF16),
                   jax.ShapeDtypeStruct((n, t * LANES, 2 * width), BF16),
                   jax.ShapeDtypeStruct((n, 2 * width, t * LANES), BF16),
                   jax.ShapeDtypeStruct((n, 1, width), F32), jax.ShapeDtypeStruct((n, 1, width), F32)],
        compiler_params=_params(("parallel",)),
        name="ssm_prep",
    )(lanes(lam_re), lanes(lam_im), lanes(jnp.repeat(log_dt, p)),
      chan(jnp.swapaxes(b_re, 1, 2)), chan(jnp.swapaxes(b_im, 1, 2)), chan(c_re), chan(c_im))


def _s5_kernel(u_ref, m_ref, wst_ref, wout_ref, lr_ref, li_ref, d_ref, y_ref, h_ref, st_ref, *, nb):
    t = SSM_CHUNK
    rows = u_ref.shape[0] // t
    half = lr_ref.shape[-1]

    @pl.when(pl.program_id(1) == 0)
    def _():
        st_ref[...] = jnp.zeros(st_ref.shape, F32)

    rp = rows // S5_SUB_BLOCKS
    parts = [slice(p * rp, (p + 1) * rp) for p in range(S5_SUB_BLOCKS)]
    u32 = [jnp.concatenate([u_ref[pl.ds(rs.start * t + s, rp, stride=t), :] for s in range(t)], axis=1)
           for rs in parts]
    u = [v.astype(BF16) for v in u32]
    for rs, v in zip(parts, u):
        h_ref[rs, :] = jnp.dot(v, wst_ref[...], preferred_element_type=F32)
    lam_r = lr_ref[...]
    lam_i = li_ref[...]

    def step(k, carry):
        sr, si = carry
        rk = pl.ds(pl.multiple_of(k * nb, nb), nb)
        hr = h_ref[rk, :half]
        hi = h_ref[rk, half:]
        h_ref[rk, :half] = sr
        h_ref[rk, half:] = si
        return (lam_r * sr - lam_i * si + hr, lam_r * si + lam_i * sr + hi)

    y = [jnp.dot(v, m_ref[...], preferred_element_type=F32) + d_ref[...] * v32
         for v, v32 in zip(u, u32)]
    sr, si = lax.fori_loop(0, rows // nb, step, (st_ref[0], st_ref[1]), unroll=True)
    st_ref[0] = sr
    st_ref[1] = si
    y = [v + jnp.dot(h_ref[rs, :].astype(BF16), wout_ref[...], preferred_element_type=F32)
         for rs, v in zip(parts, y)]
    for rs, v in zip(parts, y):
        v = jax.nn.gelu(v)
        for s in range(t):
            y_ref[pl.ds(rs.start * t + s, rp, stride=t), :] = v[:, s * LANES:(s + 1) * LANES]


def _s5(u5, toep, wst, wout, lam_r, lam_i, dvec):
    n_tiles, n_chunks, nb, t, lanes = u5.shape
    kb = min(SCAN_CHUNKS, n_chunks)
    blk_tokens = kb * nb * t
    flat = u5.reshape(n_tiles, n_chunks * nb * t, lanes)
    width = toep.shape[-1]
    st = wst.shape[-1]
    tok = pl.BlockSpec((None, blk_tokens, lanes), lambda oi, ki: (oi, ki, 0))
    op = lambda r, c: pl.BlockSpec((None, r, c), lambda oi, ki: (oi, 0, 0))
    y = pl.pallas_call(
        functools.partial(_s5_kernel, nb=nb),
        grid=(n_tiles, n_chunks // kb),
        in_specs=[tok, op(width, width), op(width, st), op(st, width), op(1, st // 2), op(1, st // 2),
                  op(1, width)],
        out_specs=tok,
        out_shape=jax.ShapeDtypeStruct(flat.shape, F32),
        scratch_shapes=[pltpu.VMEM((kb * nb, st), F32), pltpu.VMEM((2, nb, st // 2), F32)],
        compiler_params=_params(("parallel", "arbitrary")),
        name="s5",
    )(flat, toep, wst, wout, lam_r, lam_i, dvec)
    return y.reshape(u5.shape)


def _moba_kernel(qt_ref, k_ref, vt_ref, o_ref, *, n_blk, n_sel):
    blk = MOBA_BLOCK
    lanes = LANES
    hd = lanes // HEAD_PAIR
    n_heads = qt_ref.shape[1] // hd
    pair_lanes = lambda h: slice((h // HEAD_PAIR) * lanes, (h // HEAD_PAIR + 1) * lanes)
    gate_rows = 16
    means = [jnp.mean(k_ref[0, j * blk:(j + 1) * blk, :].astype(F32), axis=0, keepdims=True)
             for j in range(n_blk)]
    means.append(jnp.zeros((gate_rows - n_blk, k_ref.shape[2]), F32))
    kmean = jnp.concatenate(means, axis=0).astype(BF16)
    head_row = lax.broadcasted_iota(jnp.int32, (lanes, blk), 0) // hd
    blk_row = lax.broadcasted_iota(jnp.int32, (gate_rows, blk), 0)
    causal = (lax.broadcasted_iota(jnp.int32, (blk, blk), 0)
              <= lax.broadcasted_iota(jnp.int32, (blk, blk), 1))
    ones = jnp.ones((SUM_ROWS, blk), BF16)
    mm = lambda a, w: jnp.dot(a, w, preferred_element_type=F32)

    state = {}

    def open_unit(i, h):
        qp = qt_ref[0, pair_lanes(h), i * blk:(i + 1) * blk]
        qh = jnp.where(head_row == h % HEAD_PAIR, qp, jnp.zeros_like(qp))
        bias = [None] * i
        if i > n_sel:
            gate = mm(kmean[:, pair_lanes(h)], qh)
            for n in range(i):
                gn = gate[n:n + 1, :]
                ahead = (gate > gn) | ((gate == gn) & (blk_row < n))
                ahead = ahead & (blk_row < i)
                rank = jnp.sum(ahead.astype(F32), axis=0, keepdims=True)
                bias[n] = jnp.where(rank < n_sel, 0.0, NEG_INF)
        state[(i, h)] = [qh, bias, None, None]

    def score(i, h, j):
        if (i, h) not in state:
            open_unit(i, h)
        return mm(k_ref[0, j * blk:(j + 1) * blk, pair_lanes(h)], state[(i, h)][0])

    def absorb(i, h, j, s):
        qh, bias, m_old, acc = state[(i, h)]
        bj = bias[j] if j < i else None
        if j == i:
            s = jnp.where(causal, s, NEG_INF)
        top = jnp.max(s, axis=0, keepdims=True)
        if bj is not None:
            top = top + bj
        m = top if m_old is None else jnp.maximum(m_old, top)
        p = jnp.exp2(s - (m if bj is None else m - bj)).astype(BF16)
        v = jnp.concatenate([vt_ref[0, h * hd:(h + 1) * hd, j * blk:(j + 1) * blk], ones], axis=0)
        o = mm(v, p)
        acc = o if acc is None else acc * jnp.exp2(m_old - m) + o
        state[(i, h)] = [qh, bias, m, acc]

    units = [(i, h) for i in reversed(range(n_blk)) for h in range(n_heads)]
    steps = []
    for w in range(0, len(units), FLASH_CHAINS):
        wave = units[w:w + FLASH_CHAINS]
        for t in range(max(i for i, _ in wave) + 1):
            steps += [(i, h, i - t) for i, h in wave if t <= i]
    pending = [score(*s) for s in steps[:FLASH_LOOKAHEAD]]
    for n, (i, h, j) in enumerate(steps):
        s = pending.pop(0)
        if n + FLASH_LOOKAHEAD < len(steps):
            pending.append(score(*steps[n + FLASH_LOOKAHEAD]))
        absorb(i, h, j, s)
        if j == 0 and h % HEAD_PAIR == HEAD_PAIR - 1:
            accs = [state.pop((i, hh))[3] for hh in range(h - HEAD_PAIR + 1, h + 1)]
            ot = jnp.concatenate([a[:hd] / a[hd:hd + 1] for a in accs], axis=0)
            o_ref[0, i * blk:(i + 1) * blk, pair_lanes(h)] = ot.T.astype(BF16)


def _moba(qt, k, vt):
    b, a, s = qt.shape
    lanes = MOBA_STEP_HEADS * (a // ATTN_HEADS)
    n_blk = s // MOBA_BLOCK
    tr = pl.BlockSpec((1, lanes, s), lambda bi, hi: (bi, hi, 0))
    tok = pl.BlockSpec((1, s, lanes), lambda bi, hi: (bi, 0, hi))
    return pl.pallas_call(
        functools.partial(_moba_kernel, n_blk=n_blk, n_sel=min(MOBA_TOPK, n_blk - 1)),
        grid=(b, a // lanes),
        in_specs=[tr, tok, tr],
        out_specs=tok,
        out_shape=jax.ShapeDtypeStruct((b, s, a), BF16),
        compiler_params=_params(("parallel", "parallel")),
        name="moba",
    )(qt, k, vt)


def _round_robin(gens):
    live = list(gens)
    while live:
        for g in list(live):
            try:
                next(g)
            except StopIteration:
                live.remove(g)


def _post_kernel(yg_ref, at_ref, ga_ref, gb_ref, x_ref, p_ref, wglu_ref, bglu_ref, wa_ref, wb_ref,
                 wo_ref, gmix_ref, gpre_ref, w1_ref, w2_ref, gpost_ref, wple_ref, wpg_ref, gple_ref,
                 o_ref):
    tm = x_ref.shape[1]
    sub = min(ROW_SUB, tm)
    dff = w1_ref.shape[1]
    mm = lambda a, w: jnp.dot(a, w, preferred_element_type=F32)

    def run(rs):
        cs = slice(rs.start // SSM_CHUNK, rs.stop // SSM_CHUNK)
        yg = jnp.concatenate([yg_ref[o, cs].reshape(sub, LANES) for o in range(yg_ref.shape[0])],
                             axis=1)
        glu = mm(yg.astype(BF16), wglu_ref[...]) + bglu_ref[...]
        yb = mm(at_ref[0, rs], wb_ref[...])
        yield
        ya = mm((yg * jax.nn.sigmoid(glu)).astype(BF16), wa_ref[...])
        yield
        mixed = (ga_ref[0, rs].astype(F32) * ya + gb_ref[0, rs].astype(F32) * yb).astype(BF16)
        mo = mm(mixed, wo_ref[...])
        e = mm(p_ref[0, rs].astype(BF16), wple_ref[...])
        yield
        x = x_ref[0, rs] + _rms(mo) * gmix_ref[...]
        hm = (_rms(x) * gpre_ref[...]).astype(BF16)
        f = None
        for c in range(0, dff, FF_CHUNK):
            hid = mm(hm, w1_ref[:, c:c + FF_CHUNK])
            yield
            part = mm(jnp.square(jnp.maximum(hid, 0.0)).astype(BF16), w2_ref[c:c + FF_CHUNK, :])
            f = part if f is None else f + part
            yield
        x = x + _rms(f) * gpost_ref[...]
        gate = mm(x.astype(BF16), wpg_ref[...])
        yield
        o_ref[0, rs] = x + _rms(e * jax.nn.sigmoid(gate)) * gple_ref[...]

    _round_robin([run(slice(r, r + sub)) for r in range(0, tm, sub)])


def _post(yg5, at, ga, gb, x, p, consts):
    b, s, d = x.shape
    tm = min(POST_TILE, s)
    tok = lambda w: pl.BlockSpec((1, tm, w), lambda bi, ti: (bi, ti, 0))
    return pl.pallas_call(
        _post_kernel,
        grid=(b, s // tm),
        in_specs=[_chunk_tile_spec(yg5.shape[0], tm), tok(at.shape[2]), tok(d), tok(d), tok(d),
                  tok(p.shape[2])] + [_const_spec(c.shape) for c in consts],
        out_specs=tok(d),
        out_shape=jax.ShapeDtypeStruct((b, s, d), F32),
        compiler_params=_params(("parallel", "parallel")),
        name="post",
    )(yg5, at, ga, gb, x, p, *consts)


def kernel(x, p, g_pre_mix, w_in, ssm_lam_re, ssm_lam_im, ssm_log_dt, ssm_b_re, ssm_b_im, ssm_c_re,
           ssm_c_im, ssm_d, w_glu, b_glu, w_branch_a, w_branch_b, w_out, g_post_mix, g_pre_mlp,
           w_mlp1, w_mlp2, g_post_mlp, w_ple, w_ple_gate, g_ple):
    b, s, d = x.shape
    depth = w_in.shape[0]
    wdt = w_glu.shape[1]
    a = w_branch_b.shape[1]
    hd = a // ATTN_HEADS
    assert hd * HEAD_PAIR == LANES and ATTN_HEADS % MOBA_STEP_HEADS == 0
    assert wdt % LANES == 0 and s % MOBA_BLOCK == 0 and s % min(TOKEN_TILE, s) == 0
    assert (s // SSM_CHUNK) % min(SCAN_CHUNKS, s // SSM_CHUNK) == 0
    row = lambda v: v.reshape(1, -1)
    for i in range(depth):
        w = w_in[i].astype(BF16)
        o_q, o_k, o_v, o_g = wdt, wdt + a, wdt + 2 * a, wdt + 3 * a
        u5, qt, k, vt, ga, gb = _in_proj(
            x, row(g_pre_mix[i]), w[:, :o_q], w[:, o_q:o_k].T, w[:, o_k:o_v], w[:, o_v:o_g].T,
            w[:, o_g:], float(hd) ** -0.5 * LOG2E)

        prep = _ssm_prep(ssm_lam_re[i], ssm_lam_im[i], ssm_log_dt[i], ssm_b_re[i], ssm_b_im[i],
                         ssm_c_re[i], ssm_c_im[i])
        dvec = jnp.tile(ssm_d[i].reshape(-1, 1, LANES), (1, 1, SSM_CHUNK))
        yg5 = _s5(u5, *prep, dvec)

        at = _moba(qt, k, vt)

        x = _post(yg5, at, ga, gb, x, p[i],
                  [w_glu[i].astype(BF16), row(b_glu[i]), w_branch_a[i].astype(BF16),
                   w_branch_b[i].astype(BF16), w_out[i].astype(BF16), row(g_post_mix[i]),
                   row(g_pre_mlp[i]), w_mlp1[i].astype(BF16), w_mlp2[i].astype(BF16),
                   row(g_post_mlp[i]), w_ple[i].astype(BF16), w_ple_gate[i].astype(BF16),
                   row(g_ple[i])])
    return x
```

```python
import functools

import jax
import jax.numpy as jnp
from jax import lax
from jax.experimental import pallas as pl
from jax.experimental.pallas import tpu as pltpu

F32 = jnp.float32
BF16 = jnp.bfloat16

SSM_GROUP = 16
SSM_STATE = 64
ATTN_HEADS = 8
MOBA_BLOCK = 256
MOBA_TOPK = 3
RMS_EPS = 1e-6
NEG_INF = -1e30
LOG2E = 1.4426950408889634

LANES = 128
F32_SUBLANES = 8
VMEM_LIMIT = 48 * 1024 * 1024

SSM_CHUNK = F32_SUBLANES
TILE_GROUPS = LANES // SSM_GROUP
SCAN_CHUNKS = 32
S5_SUB_BLOCKS = 4
HEAD_PAIR = 2
MOBA_STEP_HEADS = 4
SUM_ROWS = 16
FLASH_CHAINS = 4
FLASH_LOOKAHEAD = 5
TOKEN_TILE = 1024
POST_TILE = 512
ROW_SUB = 256
FF_CHUNK = 1024

_NT = (((1,), (1,)), ((), ()))


def _rms(v):
    return v * lax.rsqrt(jnp.mean(v * v, axis=-1, keepdims=True) + RMS_EPS)


def _const_spec(shape):
    nd = len(shape)
    return pl.BlockSpec(shape, lambda *_: (0,) * nd, pipeline_mode=pl.Buffered(1))


def _params(sem):
    return pltpu.CompilerParams(dimension_semantics=sem, vmem_limit_bytes=VMEM_LIMIT)


def _chunk_tile_spec(n_tiles, tm):
    return pl.BlockSpec((n_tiles, tm // SSM_CHUNK, None, SSM_CHUNK, LANES),
                        lambda bi, ti: (0, ti, bi, 0, 0))


def _in_proj_kernel(x_ref, g_ref, wu_ref, wqt_ref, wk_ref, wvt_ref, wg_ref,
                    u_ref, qt_ref, k_ref, vt_ref, ga_ref, gb_ref, km_ref, *, q_scale):
    tm, d = x_ref.shape[1], x_ref.shape[2]
    sub = MOBA_BLOCK
    tiles = [slice(r, r + sub) for r in range(0, tm, sub)]
    norm = lambda rs: (_rms(x_ref[0, rs]) * g_ref[...]).astype(BF16)
    hbs = [norm(tiles[0])]
    for n, rs in enumerate(tiles):
        hb = hbs[n]
        if n + 1 < len(tiles):
            hbs.append(norm(tiles[n + 1]))
        cs = slice(rs.start // SSM_CHUNK, rs.stop // SSM_CHUNK)
        gates = jax.nn.sigmoid(jnp.dot(hb, wg_ref[...], preferred_element_type=F32))
        ga_ref[0, rs] = gates[:, :d].astype(BF16)
        gb_ref[0, rs] = gates[:, d:].astype(BF16)
        u = jnp.dot(hb, wu_ref[...], preferred_element_type=F32)
        for o in range(u_ref.shape[0]):
            u_ref[o, cs] = u[:, o * LANES:(o + 1) * LANES].reshape(sub // SSM_CHUNK, SSM_CHUNK, LANES)
        kf = jnp.dot(hb, wk_ref[...], preferred_element_type=F32)
        k_ref[0, rs] = kf.astype(BF16)
        km_ref[0, n] = jnp.mean(kf, axis=0, keepdims=True)
        qt = lax.dot_general(wqt_ref[...], hb, _NT, preferred_element_type=F32)
        qt_ref[0, :, rs] = (qt * q_scale).astype(BF16)
        vt_ref[0, :, rs] = lax.dot_general(wvt_ref[...], hb, _NT,
                                           preferred_element_type=F32).astype(BF16)


def _in_proj(x, g, wu, wqt, wk, wvt, wg, q_scale):
    b, s, d = x.shape
    n_tiles = wu.shape[1] // LANES
    a = wk.shape[1]
    tm = min(TOKEN_TILE, s)
    tok = lambda w: pl.BlockSpec((1, tm, w), lambda bi, ti: (bi, ti, 0))
    tr = lambda w: pl.BlockSpec((1, w, tm), lambda bi, ti: (bi, 0, ti))
    return pl.pallas_call(
        functools.partial(_in_proj_kernel, q_scale=q_scale),
        grid=(b, s // tm),
        in_specs=[tok(d), _const_spec(g.shape), _const_spec(wu.shape), _const_spec(wqt.shape),
                  _const_spec(wk.shape), _const_spec(wvt.shape), _const_spec(wg.shape)],
        out_specs=[_chunk_tile_spec(n_tiles, tm), tr(a), tok(a), tr(a), tok(d), tok(d),
                   pl.BlockSpec((1, tm // MOBA_BLOCK, 1, a), lambda bi, ti: (bi, ti, 0, 0))],
        out_shape=[jax.ShapeDtypeStruct((n_tiles, s // SSM_CHUNK, b, SSM_CHUNK, LANES), F32),
                   jax.ShapeDtypeStruct((b, a, s), BF16), jax.ShapeDtypeStruct((b, s, a), BF16),
                   jax.ShapeDtypeStruct((b, a, s), BF16),
                   jax.ShapeDtypeStruct((b, s, d), BF16), jax.ShapeDtypeStruct((b, s, d), BF16),
                   jax.ShapeDtypeStruct((b, s // MOBA_BLOCK, 1, a), F32)],
        compiler_params=_params(("parallel", "parallel")),
        name="in_proj",
    )(x, g, wu, wqt, wk, wvt, wg)


def _ssm_prep_kernel(lre_ref, lim_ref, ldt_ref, btr_ref, bti_ref, cr_ref, ci_ref,
                     toep_ref, wst_ref, wout_ref, ltr_ref, lti_ref):
    t = SSM_CHUNK
    rows, width = cr_ref.shape[1], cr_ref.shape[2]
    lre = lre_ref[0]
    lim = lim_ref[0]
    dt = jnp.exp(ldt_ref[0])
    ea = lre * dt
    eb = lim * dt

    def power(tau):
        mag = jnp.exp(tau * ea)
        return mag * jnp.cos(tau * eb), mag * jnp.sin(tau * eb)

    def times(pw, mr, mi):
        return pw[0] * mr - pw[1] * mi, pw[0] * mi + pw[1] * mr

    pw = [power(float(tau)) for tau in range(t + 1)]
    ltr_ref[0], lti_ref[0] = pw[t]
    same_group = (lax.broadcasted_iota(jnp.int32, (rows, width), 0) // SSM_GROUP
                  == lax.broadcasted_iota(jnp.int32, (rows, width), 1) // SSM_STATE)
    keep = lambda ref: jnp.where(same_group, ref[0], 0.0)
    cr, ci = keep(cr_ref), keep(ci_ref)
    nr = pw[1][0] - 1.0
    ni = pw[1][1]
    den = lre * lre + lim * lim
    coef = ((nr * lre + ni * lim) / den, (ni * lre - nr * lim) / den)
    bbr, bbi = times(coef, keep(btr_ref), keep(bti_ref))

    for s in range(t):
        wr, wi = times(pw[t - 1 - s], bbr, bbi)
        wst_ref[0, s * rows:(s + 1) * rows, :] = jnp.concatenate([wr, wi], axis=1).astype(BF16)
        orr, oi = times(pw[s + 1], cr, ci)
        wout_ref[0, :width, s * rows:(s + 1) * rows] = orr.T.astype(BF16)
        wout_ref[0, width:, s * rows:(s + 1) * rows] = (-oi).T.astype(BF16)

    cl = [times(pw[tau], cr, ci) for tau in range(t)]
    clr = jnp.concatenate([m[0] for m in cl], axis=0)
    cli = jnp.concatenate([m[1] for m in cl], axis=0)
    hi = lax.Precision.HIGHEST
    bd = (lax.dot_general(bbr, clr, _NT, precision=hi, preferred_element_type=F32)
          - lax.dot_general(bbi, cli, _NT, precision=hi, preferred_element_type=F32)).astype(BF16)
    for s in range(t):
        lead = [jnp.zeros((rows, s * rows), BF16)] if s else []
        toep_ref[0, s * rows:(s + 1) * rows, :] = jnp.concatenate(
            lead + [bd[:, :(t - s) * rows]], axis=1)


def _ssm_prep(lam_re, lam_im, log_dt, b_re, b_im, c_re, c_im):
    g, p = lam_re.shape
    tg, t = TILE_GROUPS, SSM_CHUNK
    n = g // tg
    width = tg * p
    lanes = lambda v: v.reshape(n, 1, width)
    chan = lambda m: jnp.tile(m.reshape(n, LANES, p), (1, 1, tg))
    vec = pl.BlockSpec((1, 1, width), lambda ti: (ti, 0, 0))
    mat = pl.BlockSpec((1, LANES, width), lambda ti: (ti, 0, 0))
    sq = lambda r, c: pl.BlockSpec((1, r, c), lambda ti: (ti, 0, 0))
    return pl.pallas_call(
        _ssm_prep_kernel,
        grid=(n,),
        in_specs=[vec, vec, vec, mat, mat, mat, mat],
        out_specs=[sq(t * LANES, t * LANES), sq(t * LANES, 2 * width), sq(2 * width, t * LANES),
                   vec, vec],
        out_shape=[jax.ShapeDtypeStruct((n, t * LANES, t * LANES), BF16),
                   jax.ShapeDtypeStruct((n, t * LANES, 2 * width), BF16),
                   jax.ShapeDtypeStruct((n, 2 * width, t * LANES), BF16),
                   jax.ShapeDtypeStruct((n, 1, width), F32), jax.ShapeDtypeStruct((n, 1, width), F32)],
        compiler_params=_params(("parallel",)),
        name="ssm_prep",
    )(lanes(lam_re), lanes(lam_im), lanes(jnp.repeat(log_dt, p)),
      chan(jnp.swapaxes(b_re, 1, 2)), chan(jnp.swapaxes(b_im, 1, 2)), chan(c_re), chan(c_im))


def _s5_kernel(u_ref, m_ref, wst_ref, wout_ref, lr_ref, li_ref, d_ref, y_ref, h_ref, st_ref, *, nb):
    t = SSM_CHUNK
    rows = u_ref.shape[0] // t
    half = lr_ref.shape[-1]

    @pl.when(pl.program_id(1) == 0)
    def _():
        st_ref[...] = jnp.zeros(st_ref.shape, F32)

    rp = rows // S5_SUB_BLOCKS
    parts = [slice(p * rp, (p + 1) * rp) for p in range(S5_SUB_BLOCKS)]
    u32 = [jnp.concatenate([u_ref[pl.ds(rs.start * t + s, rp, stride=t), :] for s in range(t)], axis=1)
           for rs in parts]
    u = [v.astype(BF16) for v in u32]
    for rs, v in zip(parts, u):
        h_ref[rs, :] = jnp.dot(v, wst_ref[...], preferred_element_type=F32)
    lam_r = lr_ref[...]
    lam_i = li_ref[...]

    def step(k, carry):
        sr, si = carry
        rk = pl.ds(pl.multiple_of(k * nb, nb), nb)
        hr = h_ref[rk, :half]
        hi = h_ref[rk, half:]
        h_ref[rk, :half] = sr
        h_ref[rk, half:] = si
        return (lam_r * sr - lam_i * si + hr, lam_r * si + lam_i * sr + hi)

    y = [jnp.dot(v, m_ref[...], preferred_element_type=F32) + d_ref[...] * v32
         for v, v32 in zip(u, u32)]
    sr, si = lax.fori_loop(0, rows // nb, step, (st_ref[0], st_ref[1]), unroll=True)
    st_ref[0] = sr
    st_ref[1] = si
    y = [v + jnp.dot(h_ref[rs, :].astype(BF16), wout_ref[...], preferred_element_type=F32)
         for rs, v in zip(parts, y)]
    for rs, v in zip(parts, y):
        v = jax.nn.gelu(v)
        for s in range(t):
            y_ref[pl.ds(rs.start * t + s, rp, stride=t), :] = v[:, s * LANES:(s + 1) * LANES]


def _s5(u5, toep, wst, wout, lam_r, lam_i, dvec):
    n_tiles, n_chunks, nb, t, lanes = u5.shape
    kb = min(SCAN_CHUNKS, n_chunks)
    blk_tokens = kb * nb * t
    flat = u5.reshape(n_tiles, n_chunks * nb * t, lanes)
    width = toep.shape[-1]
    st = wst.shape[-1]
    tok = pl.BlockSpec((None, blk_tokens, lanes), lambda oi, ki: (oi, ki, 0))
    op = lambda r, c: pl.BlockSpec((None, r, c), lambda oi, ki: (oi, 0, 0))
    y = pl.pallas_call(
        functools.partial(_s5_kernel, nb=nb),
        grid=(n_tiles, n_chunks // kb),
        in_specs=[tok, op(width, width), op(width, st), op(st, width), op(1, st // 2), op(1, st // 2),
                  op(1, width)],
        out_specs=tok,
        out_shape=jax.ShapeDtypeStruct(flat.shape, F32),
        scratch_shapes=[pltpu.VMEM((kb * nb, st), F32), pltpu.VMEM((2, nb, st // 2), F32)],
        compiler_params=_params(("parallel", "arbitrary")),
        name="s5",
    )(flat, toep, wst, wout, lam_r, lam_i, dvec)
    return y.reshape(u5.shape)


def _moba_kernel(qt_ref, k_ref, vt_ref, km_ref, o_ref, *, n_blk, n_sel):
    blk = MOBA_BLOCK
    lanes = LANES
    hd = lanes // HEAD_PAIR
    n_heads = qt_ref.shape[1] // hd
    pair_lanes = lambda h: slice((h // HEAD_PAIR) * lanes, (h // HEAD_PAIR + 1) * lanes)
    gate_rows = 16
    means = [km_ref[0, j] for j in range(n_blk)]
    means.append(jnp.zeros((gate_rows - n_blk, k_ref.shape[2]), F32))
    kmean = jnp.concatenate(means, axis=0).astype(BF16)
    head_row = lax.broadcasted_iota(jnp.int32, (lanes, blk), 0) // hd
    blk_row = lax.broadcasted_iota(jnp.int32, (n_blk, blk), 0)
    causal = (lax.broadcasted_iota(jnp.int32, (blk, blk), 0)
              <= lax.broadcasted_iota(jnp.int32, (blk, blk), 1))
    ones = jnp.ones((SUM_ROWS, blk), BF16)
    mm = lambda a, w: jnp.dot(a, w, preferred_element_type=F32)

    state = {}

    def open_unit(i, h):
        qp = qt_ref[0, pair_lanes(h), i * blk:(i + 1) * blk]
        qh = jnp.where(head_row == h % HEAD_PAIR, qp, jnp.zeros_like(qp))
        bias = [None] * i
        if i > n_sel:
            gate = mm(kmean[:, pair_lanes(h)], qh)[:n_blk]
            gate = jnp.where(blk_row < i, gate, -jnp.inf)
            for n in range(i):
                gn = gate[n:n + 1, :]
                ahead = (gate > gn) | ((gate == gn) & (blk_row < n))
                rank = jnp.sum(ahead.astype(F32), axis=0, keepdims=True)
                bias[n] = jnp.where(rank < n_sel, 0.0, NEG_INF)
        state[(i, h)] = [qh, bias, None, None]

    def score(i, h, j):
        if (i, h) not in state:
            open_unit(i, h)
        return mm(k_ref[0, j * blk:(j + 1) * blk, pair_lanes(h)], state[(i, h)][0])

    def absorb(i, h, j, s):
        qh, bias, m_old, acc = state[(i, h)]
        bj = bias[j] if j < i else None
        if j == i:
            s = jnp.where(causal, s, NEG_INF)
        top = jnp.max(s, axis=0, keepdims=True)
        if bj is not None:
            top = top + bj
        m = top if m_old is None else jnp.maximum(m_old, top)
        p = jnp.exp2(s - (m if bj is None else m - bj)).astype(BF16)
        v = jnp.concatenate([vt_ref[0, h * hd:(h + 1) * hd, j * blk:(j + 1) * blk], ones], axis=0)
        o = mm(v, p)
        acc = o if acc is None else acc * jnp.exp2(m_old - m) + o
        state[(i, h)] = [qh, bias, m, acc]

    units = [(i, h) for i in reversed(range(n_blk)) for h in range(n_heads)]
    steps = []
    for w in range(0, len(units), FLASH_CHAINS):
        wave = units[w:w + FLASH_CHAINS]
        for t in range(max(i for i, _ in wave) + 1):
            steps += [(i, h, i - t) for i, h in wave if t <= i]
    pending = [score(*s) for s in steps[:FLASH_LOOKAHEAD]]
    for n, (i, h, j) in enumerate(steps):
        s = pending.pop(0)
        if n + FLASH_LOOKAHEAD < len(steps):
            pending.append(score(*steps[n + FLASH_LOOKAHEAD]))
        absorb(i, h, j, s)
        if j == 0 and h % HEAD_PAIR == HEAD_PAIR - 1:
            accs = [state.pop((i, hh))[3] for hh in range(h - HEAD_PAIR + 1, h + 1)]
            ot = jnp.concatenate([a[:hd] / a[hd:hd + 1] for a in accs], axis=0)
            o_ref[0, i * blk:(i + 1) * blk, pair_lanes(h)] = ot.T.astype(BF16)


def _moba(qt, k, vt, kmean):
    b, a, s = qt.shape
    lanes = MOBA_STEP_HEADS * (a // ATTN_HEADS)
    n_blk = s // MOBA_BLOCK
    tr = pl.BlockSpec((1, lanes, s), lambda bi, hi: (bi, hi, 0))
    tok = pl.BlockSpec((1, s, lanes), lambda bi, hi: (bi, 0, hi))
    return pl.pallas_call(
        functools.partial(_moba_kernel, n_blk=n_blk, n_sel=min(MOBA_TOPK, n_blk - 1)),
        grid=(b, a // lanes),
        in_specs=[tr, tok, tr, pl.BlockSpec((1, n_blk, 1, lanes), lambda bi, hi: (bi, 0, 0, hi))],
        out_specs=tok,
        out_shape=jax.ShapeDtypeStruct((b, s, a), BF16),
        compiler_params=_params(("parallel", "parallel")),
        name="moba",
    )(qt, k, vt, kmean)


def _round_robin(gens):
    live = list(gens)
    while live:
        for g in list(live):
            try:
                next(g)
            except StopIteration:
                live.remove(g)


def _post_kernel(yg_ref, at_ref, ga_ref, gb_ref, x_ref, p_ref, wglu_ref, bglu_ref, wa_ref, wb_ref,
                 wo_ref, gmix_ref, gpre_ref, w1_ref, w2_ref, gpost_ref, wple_ref, wpg_ref, gple_ref,
                 o_ref):
    tm = x_ref.shape[1]
    sub = min(ROW_SUB, tm)
    dff = w1_ref.shape[1]
    mm = lambda a, w: jnp.dot(a, w, preferred_element_type=F32)

    def run(rs):
        cs = slice(rs.start // SSM_CHUNK, rs.stop // SSM_CHUNK)
        yg = jnp.concatenate([yg_ref[o, cs].reshape(sub, LANES) for o in range(yg_ref.shape[0])],
                             axis=1)
        glu = mm(yg.astype(BF16), wglu_ref[...]) + bglu_ref[...]
        yb = mm(at_ref[0, rs], wb_ref[...])
        yield
        ya = mm((yg * jax.nn.sigmoid(glu)).astype(BF16), wa_ref[...])
        yield
        mixed = (ga_ref[0, rs].astype(F32) * ya + gb_ref[0, rs].astype(F32) * yb).astype(BF16)
        mo = mm(mixed, wo_ref[...])
        e = mm(p_ref[0, rs].astype(BF16), wple_ref[...])
        yield
        x = x_ref[0, rs] + _rms(mo) * gmix_ref[...]
        hm = (_rms(x) * gpre_ref[...]).astype(BF16)
        f = None
        for c in range(0, dff, FF_CHUNK):
            hid = mm(hm, w1_ref[:, c:c + FF_CHUNK])
            yield
            part = mm(jnp.square(jnp.maximum(hid, 0.0)).astype(BF16), w2_ref[c:c + FF_CHUNK, :])
            f = part if f is None else f + part
            yield
        x = x + _rms(f) * gpost_ref[...]
        gate = mm(x.astype(BF16), wpg_ref[...])
        yield
        o_ref[0, rs] = x + _rms(e * jax.nn.sigmoid(gate)) * gple_ref[...]

    _round_robin([run(slice(r, r + sub)) for r in range(0, tm, sub)])


def _post(yg5, at, ga, gb, x, p, consts):
    b, s, d = x.shape
    tm = min(POST_TILE, s)
    tok = lambda w: pl.BlockSpec((1, tm, w), lambda bi, ti: (bi, ti, 0))
    return pl.pallas_call(
        _post_kernel,
        grid=(b, s // tm),
        in_specs=[_chunk_tile_spec(yg5.shape[0], tm), tok(at.shape[2]), tok(d), tok(d), tok(d),
                  tok(p.shape[2])] + [_const_spec(c.shape) for c in consts],
        out_specs=tok(d),
        out_shape=jax.ShapeDtypeStruct((b, s, d), F32),
        compiler_params=_params(("parallel", "parallel")),
        name="post",
    )(yg5, at, ga, gb, x, p, *consts)


def kernel(x, p, g_pre_mix, w_in, ssm_lam_re, ssm_lam_im, ssm_log_dt, ssm_b_re, ssm_b_im, ssm_c_re,
           ssm_c_im, ssm_d, w_glu, b_glu, w_branch_a, w_branch_b, w_out, g_post_mix, g_pre_mlp,
           w_mlp1, w_mlp2, g_post_mlp, w_ple, w_ple_gate, g_ple):
    b, s, d = x.shape
    depth = w_in.shape[0]
    wdt = w_glu.shape[1]
    a = w_branch_b.shape[1]
    hd = a // ATTN_HEADS
    assert hd * HEAD_PAIR == LANES and ATTN_HEADS % MOBA_STEP_HEADS == 0
    assert wdt % LANES == 0 and s % MOBA_BLOCK == 0 and s % min(TOKEN_TILE, s) == 0
    assert (s // SSM_CHUNK) % min(SCAN_CHUNKS, s // SSM_CHUNK) == 0
    row = lambda v: v.reshape(1, -1)
    for i in range(depth):
        w = w_in[i].astype(BF16)
        o_q, o_k, o_v, o_g = wdt, wdt + a, wdt + 2 * a, wdt + 3 * a
        u5, qt, k, vt, ga, gb, kmean = _in_proj(
            x, row(g_pre_mix[i]), w[:, :o_q], w[:, o_q:o_k].T, w[:, o_k:o_v], w[:, o_v:o_g].T,
            w[:, o_g:], float(hd) ** -0.5 * LOG2E)

        prep = _ssm_prep(ssm_lam_re[i], ssm_lam_im[i], ssm_log_dt[i], ssm_b_re[i], ssm_b_im[i],
                         ssm_c_re[i], ssm_c_im[i])
        dvec = jnp.tile(ssm_d[i].reshape(-1, 1, LANES), (1, 1, SSM_CHUNK))
        yg5 = _s5(u5, *prep, dvec)

        at = _moba(qt, k, vt, kmean)

        x = _post(yg5, at, ga, gb, x, p[i],
                  [w_glu[i].astype(BF16), row(b_glu[i]), w_branch_a[i].astype(BF16),
                   w_branch_b[i].astype(BF16), w_out[i].astype(BF16), row(g_post_mix[i]),
                   row(g_pre_mlp[i]), w_mlp1[i].astype(BF16), w_mlp2[i].astype(BF16),
                   row(g_post_mlp[i]), w_ple[i].astype(BF16), w_ple_gate[i].astype(BF16),
                   row(g_ple[i])])
    return x
```

```python
import functools

import jax
import jax.numpy as jnp
from jax import lax
from jax.experimental import pallas as pl
from jax.experimental.pallas import tpu as pltpu

F32 = jnp.float32
BF16 = jnp.bfloat16

SSM_GROUP = 16
SSM_STATE = 64
ATTN_HEADS = 8
MOBA_BLOCK = 256
MOBA_TOPK = 3
RMS_EPS = 1e-6
NEG_INF = -1e30
LOG2E = 1.4426950408889634

LANES = 128
F32_SUBLANES = 8
VMEM_LIMIT = 48 * 1024 * 1024

SSM_CHUNK = F32_SUBLANES
TILE_GROUPS = LANES // SSM_GROUP
SCAN_CHUNKS = 32
S5_SUB_BLOCKS = 4
HEAD_PAIR = 2
MOBA_STEP_HEADS = 4
SUM_ROWS = 16
FLASH_CHAINS = 4
FLASH_LOOKAHEAD = 5
TOKEN_TILE = 1024
POST_TILE = 512
ROW_SUB = 256
FF_CHUNK = 1024

_NT = (((1,), (1,)), ((), ()))


def _rms(v):
    return v * lax.rsqrt(jnp.mean(v * v, axis=-1, keepdims=True) + RMS_EPS)


def _const_spec(shape):
    nd = len(shape)
    return pl.BlockSpec(shape, lambda *_: (0,) * nd, pipeline_mode=pl.Buffered(1))


def _params(sem):
    return pltpu.CompilerParams(dimension_semantics=sem, vmem_limit_bytes=VMEM_LIMIT)


def _chunk_tile_spec(n_tiles, tm):
    return pl.BlockSpec((n_tiles, tm // SSM_CHUNK, None, SSM_CHUNK, LANES),
                        lambda bi, ti: (0, ti, bi, 0, 0))


def _in_proj_kernel(x_ref, g_ref, wu_ref, wqt_ref, wk_ref, wvt_ref, wg_ref,
                    u_ref, qt_ref, k_ref, vt_ref, ga_ref, gb_ref, km_ref, *, q_scale):
    tm, d = x_ref.shape[1], x_ref.shape[2]
    sub = MOBA_BLOCK
    tiles = [slice(r, r + sub) for r in range(0, tm, sub)]
    norm = lambda rs: (_rms(x_ref[0, rs]) * g_ref[...]).astype(BF16)
    hbs = [norm(tiles[0])]
    for n, rs in enumerate(tiles):
        hb = hbs[n]
        if n + 1 < len(tiles):
            hbs.append(norm(tiles[n + 1]))
        cs = slice(rs.start // SSM_CHUNK, rs.stop // SSM_CHUNK)
        gates = jax.nn.sigmoid(jnp.dot(hb, wg_ref[...], preferred_element_type=F32))
        ga_ref[0, rs] = gates[:, :d].astype(BF16)
        gb_ref[0, rs] = gates[:, d:].astype(BF16)
        u = jnp.dot(hb, wu_ref[...], preferred_element_type=F32)
        for o in range(u_ref.shape[0]):
            u_ref[o, cs] = u[:, o * LANES:(o + 1) * LANES].reshape(sub // SSM_CHUNK, SSM_CHUNK, LANES)
        kf = jnp.dot(hb, wk_ref[...], preferred_element_type=F32)
        k_ref[0, rs] = kf.astype(BF16)
        km_ref[0, n] = jnp.mean(kf, axis=0, keepdims=True)
        qt = lax.dot_general(wqt_ref[...], hb, _NT, preferred_element_type=F32)
        qt_ref[0, :, rs] = (qt * q_scale).astype(BF16)
        vt_ref[0, :, rs] = lax.dot_general(wvt_ref[...], hb, _NT,
                                           preferred_element_type=F32).astype(BF16)


def _in_proj(x, g, wu, wqt, wk, wvt, wg, q_scale):
    b, s, d = x.shape
    n_tiles = wu.shape[1] // LANES
    a = wk.shape[1]
    tm = min(TOKEN_TILE, s)
    tok = lambda w: pl.BlockSpec((1, tm, w), lambda bi, ti: (bi, ti, 0))
    tr = lambda w: pl.BlockSpec((1, w, tm), lambda bi, ti: (bi, 0, ti))
    return pl.pallas_call(
        functools.partial(_in_proj_kernel, q_scale=q_scale),
        grid=(b, s // tm),
        in_specs=[tok(d), _const_spec(g.shape), _const_spec(wu.shape), _const_spec(wqt.shape),
                  _const_spec(wk.shape), _const_spec(wvt.shape), _const_spec(wg.shape)],
        out_specs=[_chunk_tile_spec(n_tiles, tm), tr(a), tok(a), tr(a), tok(d), tok(d),
                   pl.BlockSpec((1, tm // MOBA_BLOCK, 1, a), lambda bi, ti: (bi, ti, 0, 0))],
        out_shape=[jax.ShapeDtypeStruct((n_tiles, s // SSM_CHUNK, b, SSM_CHUNK, LANES), F32),
                   jax.ShapeDtypeStruct((b, a, s), BF16), jax.ShapeDtypeStruct((b, s, a), BF16),
                   jax.ShapeDtypeStruct((b, a, s), BF16),
                   jax.ShapeDtypeStruct((b, s, d), BF16), jax.ShapeDtypeStruct((b, s, d), BF16),
                   jax.ShapeDtypeStruct((b, s // MOBA_BLOCK, 1, a), F32)],
        compiler_params=_params(("parallel", "parallel")),
        name="in_proj",
    )(x, g, wu, wqt, wk, wvt, wg)


def _ssm_prep_kernel(lre_ref, lim_ref, ldt_ref, btr_ref, bti_ref, cr_ref, ci_ref,
                     toep_ref, wst_ref, wout_ref, ltr_ref, lti_ref):
    t = SSM_CHUNK
    rows, width = cr_ref.shape[1], cr_ref.shape[2]
    lre = lre_ref[0]
    lim = lim_ref[0]
    dt = jnp.exp(ldt_ref[0])
    ea = lre * dt
    eb = lim * dt

    def power(tau):
        mag = jnp.exp(tau * ea)
        return mag * jnp.cos(tau * eb), mag * jnp.sin(tau * eb)

    def times(pw, mr, mi):
        return pw[0] * mr - pw[1] * mi, pw[0] * mi + pw[1] * mr

    pw = [power(float(tau)) for tau in range(t + 1)]
    ltr_ref[0], lti_ref[0] = pw[t]
    same_group = (lax.broadcasted_iota(jnp.int32, (rows, width), 0) // SSM_GROUP
                  == lax.broadcasted_iota(jnp.int32, (rows, width), 1) // SSM_STATE)
    keep = lambda ref: jnp.where(same_group, ref[0], 0.0)
    cr, ci = keep(cr_ref), keep(ci_ref)
    nr = pw[1][0] - 1.0
    ni = pw[1][1]
    den = lre * lre + lim * lim
    coef = ((nr * lre + ni * lim) / den, (ni * lre - nr * lim) / den)
    bbr, bbi = times(coef, keep(btr_ref), keep(bti_ref))

    for s in range(t):
        wr, wi = times(pw[t - 1 - s], bbr, bbi)
        wst_ref[0, s * rows:(s + 1) * rows, :] = jnp.concatenate([wr, wi], axis=1).astype(BF16)
        orr, oi = times(pw[s + 1], cr, ci)
        wout_ref[0, :width, s * rows:(s + 1) * rows] = orr.T.astype(BF16)
        wout_ref[0, width:, s * rows:(s + 1) * rows] = (-oi).T.astype(BF16)

    cl = [times(pw[tau], cr, ci) for tau in range(t)]
    clr = jnp.concatenate([m[0] for m in cl], axis=0)
    cli = jnp.concatenate([m[1] for m in cl], axis=0)
    hi = lax.Precision.HIGHEST
    bd = (lax.dot_general(bbr, clr, _NT, precision=hi, preferred_element_type=F32)
          - lax.dot_general(bbi, cli, _NT, precision=hi, preferred_element_type=F32)).astype(BF16)
    for s in range(t):
        lead = [jnp.zeros((rows, s * rows), BF16)] if s else []
        toep_ref[0, s * rows:(s + 1) * rows, :] = jnp.concatenate(
            lead + [bd[:, :(t - s) * rows]], axis=1)


def _ssm_prep(lam_re, lam_im, log_dt, b_re, b_im, c_re, c_im):
    g, p = lam_re.shape
    tg, t = TILE_GROUPS, SSM_CHUNK
    n = g // tg
    width = tg * p
    lanes = lambda v: v.reshape(n, 1, width)
    chan = lambda m: jnp.tile(m.reshape(n, LANES, p), (1, 1, tg))
    vec = pl.BlockSpec((1, 1, width), lambda ti: (ti, 0, 0))
    mat = pl.BlockSpec((1, LANES, width), lambda ti: (ti, 0, 0))
    sq = lambda r, c: pl.BlockSpec((1, r, c), lambda ti: (ti, 0, 0))
    return pl.pallas_call(
        _ssm_prep_kernel,
        grid=(n,),
        in_specs=[vec, vec, vec, mat, mat, mat, mat],
        out_specs=[sq(t * LANES, t * LANES), sq(t * LANES, 2 * width), sq(2 * width, t * LANES),
                   vec, vec],
        out_shape=[jax.ShapeDtypeStruct((n, t * LANES, t * LANES), BF16),
                   jax.ShapeDtypeStruct((n, t * LANES, 2 * width), BF16),
                   jax.ShapeDtypeStruct((n, 2 * width, t * LANES), BF16),
                   jax.ShapeDtypeStruct((n, 1, width), F32), jax.ShapeDtypeStruct((n, 1, width), F32)],
        compiler_params=_params(("parallel",)),
        name="ssm_prep",
    )(lanes(lam_re), lanes(lam_im), lanes(jnp.repeat(log_dt, p)),
      chan(jnp.swapaxes(b_re, 1, 2)), chan(jnp.swapaxes(b_im, 1, 2)), chan(c_re), chan(c_im))


def _s5_kernel(u_ref, m_ref, wst_ref, wout_ref, lr_ref, li_ref, d_ref, y_ref, h_ref, st_ref, *, nb):
    t = SSM_CHUNK
    rows = u_ref.shape[0] // t
    half = lr_ref.shape[-1]

    @pl.when(pl.program_id(1) == 0)
    def _():
        st_ref[...] = jnp.zeros(st_ref.shape, F32)

    rp = rows // S5_SUB_BLOCKS
    parts = [slice(p * rp, (p + 1) * rp) for p in range(S5_SUB_BLOCKS)]
    u32 = [jnp.concatenate([u_ref[pl.ds(rs.start * t + s, rp, stride=t), :] for s in range(t)], axis=1)
           for rs in parts]
    u = [v.astype(BF16) for v in u32]
    for rs, v in zip(parts, u):
        h_ref[rs, :] = jnp.dot(v, wst_ref[...], preferred_element_type=F32)
    lam_r = lr_ref[...]
    lam_i = li_ref[...]

    def step(k, carry):
        sr, si = carry
        rk = pl.ds(pl.multiple_of(k * nb, nb), nb)
        hr = h_ref[rk, :half]
        hi = h_ref[rk, half:]
        h_ref[rk, :half] = sr
        h_ref[rk, half:] = si
        return (lam_r * sr - lam_i * si + hr, lam_r * si + lam_i * sr + hi)

    y = [jnp.dot(v, m_ref[...], preferred_element_type=F32) + d_ref[...] * v32
         for v, v32 in zip(u, u32)]
    sr, si = lax.fori_loop(0, rows // nb, step, (st_ref[0], st_ref[1]), unroll=True)
    st_ref[0] = sr
    st_ref[1] = si
    y = [v + jnp.dot(h_ref[rs, :].astype(BF16), wout_ref[...], preferred_element_type=F32)
         for rs, v in zip(parts, y)]
    for rs, v in zip(parts, y):
        v = jax.nn.gelu(v)
        for s in range(t):
            y_ref[pl.ds(rs.start * t + s, rp, stride=t), :] = v[:, s * LANES:(s + 1) * LANES]


def _s5(u5, toep, wst, wout, lam_r, lam_i, dvec):
    n_tiles, n_chunks, nb, t, lanes = u5.shape
    kb = min(SCAN_CHUNKS, n_chunks)
    blk_tokens = kb * nb * t
    flat = u5.reshape(n_tiles, n_chunks * nb * t, lanes)
    width = toep.shape[-1]
    st = wst.shape[-1]
    tok = pl.BlockSpec((None, blk_tokens, lanes), lambda oi, ki: (oi, ki, 0))
    op = lambda r, c: pl.BlockSpec((None, r, c), lambda oi, ki: (oi, 0, 0))
    y = pl.pallas_call(
        functools.partial(_s5_kernel, nb=nb),
        grid=(n_tiles, n_chunks // kb),
        in_specs=[tok, op(width, width), op(width, st), op(st, width), op(1, st // 2), op(1, st // 2),
                  op(1, width)],
        out_specs=tok,
        out_shape=jax.ShapeDtypeStruct(flat.shape, F32),
        scratch_shapes=[pltpu.VMEM((kb * nb, st), F32), pltpu.VMEM((2, nb, st // 2), F32)],
        compiler_params=_params(("parallel", "arbitrary")),
        name="s5",
    )(flat, toep, wst, wout, lam_r, lam_i, dvec)
    return y.reshape(u5.shape)


def _moba_kernel(qt_ref, k_ref, vt_ref, km_ref, o_ref, *, n_blk, n_sel):
    blk = MOBA_BLOCK
    lanes = LANES
    hd = lanes // HEAD_PAIR
    n_heads = qt_ref.shape[1] // hd
    pair_lanes = lambda h: slice((h // HEAD_PAIR) * lanes, (h // HEAD_PAIR + 1) * lanes)
    gate_rows = 16
    means = [km_ref[0, j] for j in range(n_blk)]
    means.append(jnp.zeros((gate_rows - n_blk, k_ref.shape[2]), F32))
    kmean = jnp.concatenate(means, axis=0).astype(BF16)
    head_row = lax.broadcasted_iota(jnp.int32, (lanes, blk), 0) // hd
    blk_row = lax.broadcasted_iota(jnp.int32, (n_blk, blk), 0)
    causal = (lax.broadcasted_iota(jnp.int32, (blk, blk), 0)
              <= lax.broadcasted_iota(jnp.int32, (blk, blk), 1))
    ones = jnp.ones((SUM_ROWS, blk), BF16)
    mm = lambda a, w: jnp.dot(a, w, preferred_element_type=F32)

    state = {}

    def open_unit(i, h):
        qp = qt_ref[0, pair_lanes(h), i * blk:(i + 1) * blk]
        qh = jnp.where(head_row == h % HEAD_PAIR, qp, jnp.zeros_like(qp))
        bias = [None] * i
        if i > n_sel:
            gate = mm(kmean[:, pair_lanes(h)], qh)[:n_blk]
            gate = jnp.where(blk_row < i, gate, -jnp.inf)
            for n in range(i):
                gn = gate[n:n + 1, :]
                ahead = (gate > gn) | ((gate == gn) & (blk_row < n))
                rank = jnp.sum(ahead.astype(F32), axis=0, keepdims=True)
                bias[n] = jnp.where(rank < n_sel, 0.0, NEG_INF)
        state[(i, h)] = [qh, bias, None, None]

    def score(i, h, j):
        if (i, h) not in state:
            open_unit(i, h)
        return mm(k_ref[0, j * blk:(j + 1) * blk, pair_lanes(h)], state[(i, h)][0])

    def absorb(i, h, j, s):
        qh, bias, m_old, acc = state[(i, h)]
        bj = bias[j] if j < i else None
        if j == i:
            s = jnp.where(causal, s, NEG_INF)
        top = jnp.max(s, axis=0, keepdims=True)
        if bj is not None:
            top = top + bj
        m = top if m_old is None else jnp.maximum(m_old, top)
        p = jnp.exp2(s - (m if bj is None else m - bj)).astype(BF16)
        v = jnp.concatenate([vt_ref[0, h * hd:(h + 1) * hd, j * blk:(j + 1) * blk], ones], axis=0)
        o = mm(v, p)
        acc = o if acc is None else acc * jnp.exp2(m_old - m) + o
        state[(i, h)] = [qh, bias, m, acc]

    units = [(i, h) for i in reversed(range(n_blk)) for h in range(n_heads)]
    steps = []
    for w in range(0, len(units), FLASH_CHAINS):
        wave = units[w:w + FLASH_CHAINS]
        for t in range(max(i for i, _ in wave) + 1):
            steps += [(i, h, i - t) for i, h in wave if t <= i]
    pending = [score(*s) for s in steps[:FLASH_LOOKAHEAD]]
    for n, (i, h, j) in enumerate(steps):
        s = pending.pop(0)
        if n + FLASH_LOOKAHEAD < len(steps):
            pending.append(score(*steps[n + FLASH_LOOKAHEAD]))
        absorb(i, h, j, s)
        if j == 0 and h % HEAD_PAIR == HEAD_PAIR - 1:
            accs = [state.pop((i, hh))[3] for hh in range(h - HEAD_PAIR + 1, h + 1)]
            ot = jnp.concatenate([a[:hd] / a[hd:hd + 1] for a in accs], axis=0)
            o_ref[0, i * blk:(i + 1) * blk, pair_lanes(h)] = ot.T.astype(BF16)


def _moba(qt, k, vt, kmean):
    b, a, s = qt.shape
    lanes = MOBA_STEP_HEADS * (a // ATTN_HEADS)
    n_blk = s // MOBA_BLOCK
    tr = pl.BlockSpec((1, lanes, s), lambda bi, hi: (bi, hi, 0))
    tok = pl.BlockSpec((1, s, lanes), lambda bi, hi: (bi, 0, hi))
    return pl.pallas_call(
        functools.partial(_moba_kernel, n_blk=n_blk, n_sel=min(MOBA_TOPK, n_blk - 1)),
        grid=(b, a // lanes),
        in_specs=[tr, tok, tr, pl.BlockSpec((1, n_blk, 1, lanes), lambda bi, hi: (bi, 0, 0, hi))],
        out_specs=tok,
        out_shape=jax.ShapeDtypeStruct((b, s, a), BF16),
        compiler_params=_params(("parallel", "parallel")),
        name="moba",
    )(qt, k, vt, kmean)


def _post_kernel(yg_ref, at_ref, ga_ref, gb_ref, x_ref, p_ref, wglu_ref, bglu_ref, wa_ref, wb_ref,
                 wo_ref, gmix_ref, gpre_ref, w1_ref, w2_ref, gpost_ref, wple_ref, wpg_ref, gple_ref,
                 o_ref):
    tm = x_ref.shape[1]
    sub = min(ROW_SUB, tm)
    dff = w1_ref.shape[1]
    mm = lambda a, w: jnp.dot(a, w, preferred_element_type=F32)
    tiles = [slice(r, r + sub) for r in range(0, tm, sub)]
    chunks = [slice(rs.start // SSM_CHUNK, rs.stop // SSM_CHUNK) for rs in tiles]
    yg = [jnp.concatenate([yg_ref[o, cs].reshape(sub, LANES) for o in range(yg_ref.shape[0])], axis=1)
          for cs in chunks]
    glu = [mm(v.astype(BF16), wglu_ref[...]) + bglu_ref[...] for v in yg]
    yb = [mm(at_ref[0, rs], wb_ref[...]) for rs in tiles]
    ya_in = [(v * jax.nn.sigmoid(s)).astype(BF16) for v, s in zip(yg, glu)]
    ya = [mm(v, wa_ref[...]) for v in ya_in]
    mixed = [(ga_ref[0, rs].astype(F32) * a + gb_ref[0, rs].astype(F32) * b).astype(BF16)
             for rs, a, b in zip(tiles, ya, yb)]
    mo = [mm(v, wo_ref[...]) for v in mixed]
    e = [mm(p_ref[0, rs].astype(BF16), wple_ref[...]) for rs in tiles]
    xs = [x_ref[0, rs] + _rms(v) * gmix_ref[...] for rs, v in zip(tiles, mo)]

    hm = [(_rms(x) * gpre_ref[...]).astype(BF16) for x in xs]
    f = [None] * len(tiles)
    for c in range(0, dff, FF_CHUNK):
        hid = [mm(h, w1_ref[:, c:c + FF_CHUNK]) for h in hm]
        hid = [jnp.square(jnp.maximum(h, 0.0)).astype(BF16) for h in hid]
        part = [mm(h, w2_ref[c:c + FF_CHUNK, :]) for h in hid]
        f = [v if acc is None else acc + v for acc, v in zip(f, part)]
    xs = [x + _rms(v) * gpost_ref[...] for x, v in zip(xs, f)]

    gate = [mm(x.astype(BF16), wpg_ref[...]) for x in xs]
    for rs, x, ev, gv in zip(tiles, xs, e, gate):
        o_ref[0, rs] = x + _rms(ev * jax.nn.sigmoid(gv)) * gple_ref[...]


def _post(yg5, at, ga, gb, x, p, consts):
    b, s, d = x.shape
    tm = min(POST_TILE, s)
    tok = lambda w: pl.BlockSpec((1, tm, w), lambda bi, ti: (bi, ti, 0))
    return pl.pallas_call(
        _post_kernel,
        grid=(b, s // tm),
        in_specs=[_chunk_tile_spec(yg5.shape[0], tm), tok(at.shape[2]), tok(d), tok(d), tok(d),
                  tok(p.shape[2])] + [_const_spec(c.shape) for c in consts],
        out_specs=tok(d),
        out_shape=jax.ShapeDtypeStruct((b, s, d), F32),
        compiler_params=_params(("parallel", "parallel")),
        name="post",
    )(yg5, at, ga, gb, x, p, *consts)


def kernel(x, p, g_pre_mix, w_in, ssm_lam_re, ssm_lam_im, ssm_log_dt, ssm_b_re, ssm_b_im, ssm_c_re,
           ssm_c_im, ssm_d, w_glu, b_glu, w_branch_a, w_branch_b, w_out, g_post_mix, g_pre_mlp,
           w_mlp1, w_mlp2, g_post_mlp, w_ple, w_ple_gate, g_ple):
    b, s, d = x.shape
    depth = w_in.shape[0]
    wdt = w_glu.shape[1]
    a = w_branch_b.shape[1]
    hd = a // ATTN_HEADS
    assert hd * HEAD_PAIR == LANES and ATTN_HEADS % MOBA_STEP_HEADS == 0
    assert wdt % LANES == 0 and s % MOBA_BLOCK == 0 and s % min(TOKEN_TILE, s) == 0
    assert (s // SSM_CHUNK) % min(SCAN_CHUNKS, s // SSM_CHUNK) == 0
    row = lambda v: v.reshape(1, -1)
    for i in range(depth):
        w = w_in[i].astype(BF16)
        o_q, o_k, o_v, o_g = wdt, wdt + a, wdt + 2 * a, wdt + 3 * a
        u5, qt, k, vt, ga, gb, kmean = _in_proj(
            x, row(g_pre_mix[i]), w[:, :o_q], w[:, o_q:o_k].T, w[:, o_k:o_v], w[:, o_v:o_g].T,
            w[:, o_g:], float(hd) ** -0.5 * LOG2E)

        prep = _ssm_prep(ssm_lam_re[i], ssm_lam_im[i], ssm_log_dt[i], ssm_b_re[i], ssm_b_im[i],
                         ssm_c_re[i], ssm_c_im[i])
        dvec = jnp.tile(ssm_d[i].reshape(-1, 1, LANES), (1, 1, SSM_CHUNK))
        yg5 = _s5(u5, *prep, dvec)

        at = _moba(qt, k, vt, kmean)

        x = _post(yg5, at, ga, gb, x, p[i],
                  [w_glu[i].astype(BF16), row(b_glu[i]), w_branch_a[i].astype(BF16),
                   w_branch_b[i].astype(BF16), w_out[i].astype(BF16), row(g_post_mix[i]),
                   row(g_pre_mlp[i]), w_mlp1[i].astype(BF16), w_mlp2[i].astype(BF16),
                   row(g_post_mlp[i]), w_ple[i].astype(BF16), w_ple_gate[i].astype(BF16),
                   row(g_ple[i])])
    return x
```

```python
import functools

import jax
import jax.numpy as jnp
from jax import lax
from jax.experimental import pallas as pl
from jax.experimental.pallas import tpu as pltpu

F32 = jnp.float32
BF16 = jnp.bfloat16

SSM_GROUP = 16
SSM_STATE = 64
ATTN_HEADS = 8
MOBA_BLOCK = 256
MOBA_TOPK = 3
RMS_EPS = 1e-6
NEG_INF = -1e30
LOG2E = 1.4426950408889634

LANES = 128
F32_SUBLANES = 8
VMEM_LIMIT = 48 * 1024 * 1024

SSM_CHUNK = F32_SUBLANES
TILE_GROUPS = LANES // SSM_GROUP
SCAN_CHUNKS = 32
S5_SUB_BLOCKS = 4
HEAD_PAIR = 2
MOBA_STEP_HEADS = 8
SUM_ROWS = 16
FLASH_CHAINS = 4
FLASH_LOOKAHEAD = 5
TOKEN_TILE = 1024
POST_TILE = 512
ROW_SUB = 256
FF_CHUNK = 1024

_NT = (((1,), (1,)), ((), ()))


def _rms(v):
    return v * lax.rsqrt(jnp.mean(v * v, axis=-1, keepdims=True) + RMS_EPS)


def _const_spec(shape):
    nd = len(shape)
    return pl.BlockSpec(shape, lambda *_: (0,) * nd, pipeline_mode=pl.Buffered(1))


def _params(sem):
    return pltpu.CompilerParams(dimension_semantics=sem, vmem_limit_bytes=VMEM_LIMIT)


def _chunk_tile_spec(n_tiles, tm):
    return pl.BlockSpec((n_tiles, tm // SSM_CHUNK, None, SSM_CHUNK, LANES),
                        lambda bi, ti: (0, ti, bi, 0, 0))


def _in_proj_kernel(x_ref, g_ref, wu_ref, wqt_ref, wk_ref, wvt_ref, wg_ref,
                    u_ref, qt_ref, k_ref, vt_ref, ga_ref, gb_ref, km_ref, *, q_scale):
    tm, d = x_ref.shape[1], x_ref.shape[2]
    sub = MOBA_BLOCK
    tiles = [slice(r, r + sub) for r in range(0, tm, sub)]
    norm = lambda rs: (_rms(x_ref[0, rs]) * g_ref[...]).astype(BF16)
    hbs = [norm(tiles[0])]
    for n, rs in enumerate(tiles):
        hb = hbs[n]
        if n + 1 < len(tiles):
            hbs.append(norm(tiles[n + 1]))
        cs = slice(rs.start // SSM_CHUNK, rs.stop // SSM_CHUNK)
        gates = jax.nn.sigmoid(jnp.dot(hb, wg_ref[...], preferred_element_type=F32))
        ga_ref[0, rs] = gates[:, :d].astype(BF16)
        gb_ref[0, rs] = gates[:, d:].astype(BF16)
        u = jnp.dot(hb, wu_ref[...], preferred_element_type=F32)
        for o in range(u_ref.shape[0]):
            u_ref[o, cs] = u[:, o * LANES:(o + 1) * LANES].reshape(sub // SSM_CHUNK, SSM_CHUNK, LANES)
        kf = jnp.dot(hb, wk_ref[...], preferred_element_type=F32)
        k_ref[0, rs] = kf.astype(BF16)
        km_ref[0, n] = jnp.mean(kf, axis=0, keepdims=True)
        qt = lax.dot_general(wqt_ref[...], hb, _NT, preferred_element_type=F32)
        qt_ref[0, :, rs] = (qt * q_scale).astype(BF16)
        vt_ref[0, :, rs] = lax.dot_general(wvt_ref[...], hb, _NT,
                                           preferred_element_type=F32).astype(BF16)


def _in_proj(x, g, wu, wqt, wk, wvt, wg, q_scale):
    b, s, d = x.shape
    n_tiles = wu.shape[1] // LANES
    a = wk.shape[1]
    tm = min(TOKEN_TILE, s)
    tok = lambda w: pl.BlockSpec((1, tm, w), lambda bi, ti: (bi, ti, 0))
    tr = lambda w: pl.BlockSpec((1, w, tm), lambda bi, ti: (bi, 0, ti))
    return pl.pallas_call(
        functools.partial(_in_proj_kernel, q_scale=q_scale),
        grid=(b, s // tm),
        in_specs=[tok(d), _const_spec(g.shape), _const_spec(wu.shape), _const_spec(wqt.shape),
                  _const_spec(wk.shape), _const_spec(wvt.shape), _const_spec(wg.shape)],
        out_specs=[_chunk_tile_spec(n_tiles, tm), tr(a), tok(a), tr(a), tok(d), tok(d),
                   pl.BlockSpec((1, tm // MOBA_BLOCK, 1, a), lambda bi, ti: (bi, ti, 0, 0))],
        out_shape=[jax.ShapeDtypeStruct((n_tiles, s // SSM_CHUNK, b, SSM_CHUNK, LANES), F32),
                   jax.ShapeDtypeStruct((b, a, s), BF16), jax.ShapeDtypeStruct((b, s, a), BF16),
                   jax.ShapeDtypeStruct((b, a, s), BF16),
                   jax.ShapeDtypeStruct((b, s, d), BF16), jax.ShapeDtypeStruct((b, s, d), BF16),
                   jax.ShapeDtypeStruct((b, s // MOBA_BLOCK, 1, a), F32)],
        compiler_params=_params(("parallel", "parallel")),
        name="in_proj",
    )(x, g, wu, wqt, wk, wvt, wg)


def _ssm_prep_kernel(lre_ref, lim_ref, ldt_ref, btr_ref, bti_ref, cr_ref, ci_ref,
                     toep_ref, wst_ref, wout_ref, ltr_ref, lti_ref):
    t = SSM_CHUNK
    rows, width = cr_ref.shape[1], cr_ref.shape[2]
    lre = lre_ref[0]
    lim = lim_ref[0]
    dt = jnp.exp(ldt_ref[0])
    ea = lre * dt
    eb = lim * dt

    def power(tau):
        mag = jnp.exp(tau * ea)
        return mag * jnp.cos(tau * eb), mag * jnp.sin(tau * eb)

    def times(pw, mr, mi):
        return pw[0] * mr - pw[1] * mi, pw[0] * mi + pw[1] * mr

    pw = [power(float(tau)) for tau in range(t + 1)]
    ltr_ref[0], lti_ref[0] = pw[t]
    same_group = (lax.broadcasted_iota(jnp.int32, (rows, width), 0) // SSM_GROUP
                  == lax.broadcasted_iota(jnp.int32, (rows, width), 1) // SSM_STATE)
    keep = lambda ref: jnp.where(same_group, ref[0], 0.0)
    cr, ci = keep(cr_ref), keep(ci_ref)
    nr = pw[1][0] - 1.0
    ni = pw[1][1]
    den = lre * lre + lim * lim
    coef = ((nr * lre + ni * lim) / den, (ni * lre - nr * lim) / den)
    bbr, bbi = times(coef, keep(btr_ref), keep(bti_ref))

    for s in range(t):
        wr, wi = times(pw[t - 1 - s], bbr, bbi)
        wst_ref[0, s * rows:(s + 1) * rows, :] = jnp.concatenate([wr, wi], axis=1).astype(BF16)
        orr, oi = times(pw[s + 1], cr, ci)
        wout_ref[0, :width, s * rows:(s + 1) * rows] = orr.T.astype(BF16)
        wout_ref[0, width:, s * rows:(s + 1) * rows] = (-oi).T.astype(BF16)

    cl = [times(pw[tau], cr, ci) for tau in range(t)]
    clr = jnp.concatenate([m[0] for m in cl], axis=0)
    cli = jnp.concatenate([m[1] for m in cl], axis=0)
    hi = lax.Precision.HIGHEST
    bd = (lax.dot_general(bbr, clr, _NT, precision=hi, preferred_element_type=F32)
          - lax.dot_general(bbi, cli, _NT, precision=hi, preferred_element_type=F32)).astype(BF16)
    for s in range(t):
        lead = [jnp.zeros((rows, s * rows), BF16)] if s else []
        toep_ref[0, s * rows:(s + 1) * rows, :] = jnp.concatenate(
            lead + [bd[:, :(t - s) * rows]], axis=1)


def _ssm_prep(lam_re, lam_im, log_dt, b_re, b_im, c_re, c_im):
    g, p = lam_re.shape
    tg, t = TILE_GROUPS, SSM_CHUNK
    n = g // tg
    width = tg * p
    lanes = lambda v: v.reshape(n, 1, width)
    chan = lambda m: jnp.tile(m.reshape(n, LANES, p), (1, 1, tg))
    vec = pl.BlockSpec((1, 1, width), lambda ti: (ti, 0, 0))
    mat = pl.BlockSpec((1, LANES, width), lambda ti: (ti, 0, 0))
    sq = lambda r, c: pl.BlockSpec((1, r, c), lambda ti: (ti, 0, 0))
    return pl.pallas_call(
        _ssm_prep_kernel,
        grid=(n,),
        in_specs=[vec, vec, vec, mat, mat, mat, mat],
        out_specs=[sq(t * LANES, t * LANES), sq(t * LANES, 2 * width), sq(2 * width, t * LANES),
                   vec, vec],
        out_shape=[jax.ShapeDtypeStruct((n, t * LANES, t * LANES), BF16),
                   jax.ShapeDtypeStruct((n, t * LANES, 2 * width), BF16),
                   jax.ShapeDtypeStruct((n, 2 * width, t * LANES), BF16),
                   jax.ShapeDtypeStruct((n, 1, width), F32), jax.ShapeDtypeStruct((n, 1, width), F32)],
        compiler_params=_params(("parallel",)),
        name="ssm_prep",
    )(lanes(lam_re), lanes(lam_im), lanes(jnp.repeat(log_dt, p)),
      chan(jnp.swapaxes(b_re, 1, 2)), chan(jnp.swapaxes(b_im, 1, 2)), chan(c_re), chan(c_im))


def _s5_kernel(u_ref, m_ref, wst_ref, wout_ref, lr_ref, li_ref, d_ref, y_ref, h_ref, st_ref, *, nb):
    t = SSM_CHUNK
    rows = u_ref.shape[0] // t
    half = lr_ref.shape[-1]

    @pl.when(pl.program_id(1) == 0)
    def _():
        st_ref[...] = jnp.zeros(st_ref.shape, F32)

    rp = rows // S5_SUB_BLOCKS
    parts = [slice(p * rp, (p + 1) * rp) for p in range(S5_SUB_BLOCKS)]
    u32 = [jnp.concatenate([u_ref[pl.ds(rs.start * t + s, rp, stride=t), :] for s in range(t)], axis=1)
           for rs in parts]
    u = [v.astype(BF16) for v in u32]
    for rs, v in zip(parts, u):
        h_ref[rs, :] = jnp.dot(v, wst_ref[...], preferred_element_type=F32)
    lam_r = lr_ref[...]
    lam_i = li_ref[...]

    def step(k, carry):
        sr, si = carry
        rk = pl.ds(pl.multiple_of(k * nb, nb), nb)
        hr = h_ref[rk, :half]
        hi = h_ref[rk, half:]
        h_ref[rk, :half] = sr
        h_ref[rk, half:] = si
        return (lam_r * sr - lam_i * si + hr, lam_r * si + lam_i * sr + hi)

    y = [jnp.dot(v, m_ref[...], preferred_element_type=F32) + d_ref[...] * v32
         for v, v32 in zip(u, u32)]
    sr, si = lax.fori_loop(0, rows // nb, step, (st_ref[0], st_ref[1]), unroll=True)
    st_ref[0] = sr
    st_ref[1] = si
    y = [v + jnp.dot(h_ref[rs, :].astype(BF16), wout_ref[...], preferred_element_type=F32)
         for rs, v in zip(parts, y)]
    for rs, v in zip(parts, y):
        v = jax.nn.gelu(v)
        for s in range(t):
            y_ref[pl.ds(rs.start * t + s, rp, stride=t), :] = v[:, s * LANES:(s + 1) * LANES]


def _s5(u5, toep, wst, wout, lam_r, lam_i, dvec):
    n_tiles, n_chunks, nb, t, lanes = u5.shape
    kb = min(SCAN_CHUNKS, n_chunks)
    blk_tokens = kb * nb * t
    flat = u5.reshape(n_tiles, n_chunks * nb * t, lanes)
    width = toep.shape[-1]
    st = wst.shape[-1]
    tok = pl.BlockSpec((None, blk_tokens, lanes), lambda oi, ki: (oi, ki, 0))
    op = lambda r, c: pl.BlockSpec((None, r, c), lambda oi, ki: (oi, 0, 0))
    y = pl.pallas_call(
        functools.partial(_s5_kernel, nb=nb),
        grid=(n_tiles, n_chunks // kb),
        in_specs=[tok, op(width, width), op(width, st), op(st, width), op(1, st // 2), op(1, st // 2),
                  op(1, width)],
        out_specs=tok,
        out_shape=jax.ShapeDtypeStruct(flat.shape, F32),
        scratch_shapes=[pltpu.VMEM((kb * nb, st), F32), pltpu.VMEM((2, nb, st // 2), F32)],
        compiler_params=_params(("parallel", "arbitrary")),
        name="s5",
    )(flat, toep, wst, wout, lam_r, lam_i, dvec)
    return y.reshape(u5.shape)


def _moba_kernel(qt_ref, k_ref, vt_ref, km_ref, o_ref, *, n_blk, n_sel):
    blk = MOBA_BLOCK
    lanes = LANES
    hd = lanes // HEAD_PAIR
    n_heads = qt_ref.shape[1] // hd
    pair_lanes = lambda h: slice((h // HEAD_PAIR) * lanes, (h // HEAD_PAIR + 1) * lanes)
    gate_rows = 16
    means = [km_ref[0, j] for j in range(n_blk)]
    means.append(jnp.zeros((gate_rows - n_blk, k_ref.shape[2]), F32))
    kmean = jnp.concatenate(means, axis=0).astype(BF16)
    head_row = lax.broadcasted_iota(jnp.int32, (lanes, blk), 0) // hd
    blk_row = lax.broadcasted_iota(jnp.int32, (n_blk, blk), 0)
    causal = (lax.broadcasted_iota(jnp.int32, (blk, blk), 0)
              <= lax.broadcasted_iota(jnp.int32, (blk, blk), 1))
    ones = jnp.ones((SUM_ROWS, blk), BF16)
    mm = lambda a, w: jnp.dot(a, w, preferred_element_type=F32)

    state = {}

    def open_unit(i, h):
        qp = qt_ref[0, pair_lanes(h), i * blk:(i + 1) * blk]
        qh = jnp.where(head_row == h % HEAD_PAIR, qp, jnp.zeros_like(qp))
        bias = [None] * i
        if i > n_sel:
            gate = mm(kmean[:, pair_lanes(h)], qh)[:n_blk]
            gate = jnp.where(blk_row < i, gate, -jnp.inf)
            for n in range(i):
                gn = gate[n:n + 1, :]
                ahead = (gate > gn) | ((gate == gn) & (blk_row < n))
                rank = jnp.sum(ahead.astype(F32), axis=0, keepdims=True)
                bias[n] = jnp.where(rank < n_sel, 0.0, NEG_INF)
        state[(i, h)] = [qh, bias, None, None]

    def score(i, h, j):
        if (i, h) not in state:
            open_unit(i, h)
        return mm(k_ref[0, j * blk:(j + 1) * blk, pair_lanes(h)], state[(i, h)][0])

    def absorb(i, h, j, s):
        qh, bias, m_old, acc = state[(i, h)]
        bj = bias[j] if j < i else None
        if j == i:
            s = jnp.where(causal, s, NEG_INF)
        top = jnp.max(s, axis=0, keepdims=True)
        if bj is not None:
            top = top + bj
        m = top if m_old is None else jnp.maximum(m_old, top)
        p = jnp.exp2(s - (m if bj is None else m - bj)).astype(BF16)
        v = jnp.concatenate([vt_ref[0, h * hd:(h + 1) * hd, j * blk:(j + 1) * blk], ones], axis=0)
        o = mm(v, p)
        acc = o if acc is None else acc * jnp.exp2(m_old - m) + o
        state[(i, h)] = [qh, bias, m, acc]

    units = [(i, h) for i in reversed(range(n_blk)) for h in range(n_heads)]
    steps = []
    for w in range(0, len(units), FLASH_CHAINS):
        wave = units[w:w + FLASH_CHAINS]
        for t in range(max(i for i, _ in wave) + 1):
            steps += [(i, h, i - t) for i, h in wave if t <= i]
    pending = [score(*s) for s in steps[:FLASH_LOOKAHEAD]]
    for n, (i, h, j) in enumerate(steps):
        s = pending.pop(0)
        if n + FLASH_LOOKAHEAD < len(steps):
            pending.append(score(*steps[n + FLASH_LOOKAHEAD]))
        absorb(i, h, j, s)
        if j == 0 and h % HEAD_PAIR == HEAD_PAIR - 1:
            accs = [state.pop((i, hh))[3] for hh in range(h - HEAD_PAIR + 1, h + 1)]
            ot = jnp.concatenate([a[:hd] / a[hd:hd + 1] for a in accs], axis=0)
            o_ref[0, i * blk:(i + 1) * blk, pair_lanes(h)] = ot.T.astype(BF16)


def _moba(qt, k, vt, kmean):
    b, a, s = qt.shape
    lanes = MOBA_STEP_HEADS * (a // ATTN_HEADS)
    n_blk = s // MOBA_BLOCK
    tr = pl.BlockSpec((1, lanes, s), lambda bi, hi: (bi, hi, 0))
    tok = pl.BlockSpec((1, s, lanes), lambda bi, hi: (bi, 0, hi))
    return pl.pallas_call(
        functools.partial(_moba_kernel, n_blk=n_blk, n_sel=min(MOBA_TOPK, n_blk - 1)),
        grid=(b, a // lanes),
        in_specs=[tr, tok, tr, pl.BlockSpec((1, n_blk, 1, lanes), lambda bi, hi: (bi, 0, 0, hi))],
        out_specs=tok,
        out_shape=jax.ShapeDtypeStruct((b, s, a), BF16),
        compiler_params=_params(("parallel", "parallel")),
        name="moba",
    )(qt, k, vt, kmean)


def _post_kernel(yg_ref, at_ref, ga_ref, gb_ref, x_ref, p_ref, wglu_ref, bglu_ref, wa_ref, wb_ref,
                 wo_ref, gmix_ref, gpre_ref, w1_ref, w2_ref, gpost_ref, wple_ref, wpg_ref, gple_ref,
                 o_ref):
    tm = x_ref.shape[1]
    sub = min(ROW_SUB, tm)
    dff = w1_ref.shape[1]
    mm = lambda a, w: jnp.dot(a, w, preferred_element_type=F32)
    tiles = [slice(r, r + sub) for r in range(0, tm, sub)]
    chunks = [slice(rs.start // SSM_CHUNK, rs.stop // SSM_CHUNK) for rs in tiles]
    yg = [jnp.concatenate([yg_ref[o, cs].reshape(sub, LANES) for o in range(yg_ref.shape[0])], axis=1)
          for cs in chunks]
    glu = [mm(v.astype(BF16), wglu_ref[...]) + bglu_ref[...] for v in yg]
    yb = [mm(at_ref[0, rs], wb_ref[...]) for rs in tiles]
    ya_in = [(v * jax.nn.sigmoid(s)).astype(BF16) for v, s in zip(yg, glu)]
    ya = [mm(v, wa_ref[...]) for v in ya_in]
    mixed = [(ga_ref[0, rs].astype(F32) * a + gb_ref[0, rs].astype(F32) * b).astype(BF16)
             for rs, a, b in zip(tiles, ya, yb)]
    mo = [mm(v, wo_ref[...]) for v in mixed]
    e = [mm(p_ref[0, rs].astype(BF16), wple_ref[...]) for rs in tiles]
    xs = [x_ref[0, rs] + _rms(v) * gmix_ref[...] for rs, v in zip(tiles, mo)]

    hm = [(_rms(x) * gpre_ref[...]).astype(BF16) for x in xs]
    f = [None] * len(tiles)
    for c in range(0, dff, FF_CHUNK):
        hid = [mm(h, w1_ref[:, c:c + FF_CHUNK]) for h in hm]
        hid = [jnp.square(jnp.maximum(h, 0.0)).astype(BF16) for h in hid]
        part = [mm(h, w2_ref[c:c + FF_CHUNK, :]) for h in hid]
        f = [v if acc is None else acc + v for acc, v in zip(f, part)]
    xs = [x + _rms(v) * gpost_ref[...] for x, v in zip(xs, f)]

    gate = [mm(x.astype(BF16), wpg_ref[...]) for x in xs]
    for rs, x, ev, gv in zip(tiles, xs, e, gate):
        o_ref[0, rs] = x + _rms(ev * jax.nn.sigmoid(gv)) * gple_ref[...]


def _post(yg5, at, ga, gb, x, p, consts):
    b, s, d = x.shape
    tm = min(POST_TILE, s)
    tok = lambda w: pl.BlockSpec((1, tm, w), lambda bi, ti: (bi, ti, 0))
    return pl.pallas_call(
        _post_kernel,
        grid=(b, s // tm),
        in_specs=[_chunk_tile_spec(yg5.shape[0], tm), tok(at.shape[2]), tok(d), tok(d), tok(d),
                  tok(p.shape[2])] + [_const_spec(c.shape) for c in consts],
        out_specs=tok(d),
        out_shape=jax.ShapeDtypeStruct((b, s, d), F32),
        compiler_params=_params(("parallel", "parallel")),
        name="post",
    )(yg5, at, ga, gb, x, p, *consts)


def kernel(x, p, g_pre_mix, w_in, ssm_lam_re, ssm_lam_im, ssm_log_dt, ssm_b_re, ssm_b_im, ssm_c_re,
           ssm_c_im, ssm_d, w_glu, b_glu, w_branch_a, w_branch_b, w_out, g_post_mix, g_pre_mlp,
           w_mlp1, w_mlp2, g_post_mlp, w_ple, w_ple_gate, g_ple):
    b, s, d = x.shape
    depth = w_in.shape[0]
    wdt = w_glu.shape[1]
    a = w_branch_b.shape[1]
    hd = a // ATTN_HEADS
    assert hd * HEAD_PAIR == LANES and ATTN_HEADS % MOBA_STEP_HEADS == 0
    assert wdt % LANES == 0 and s % MOBA_BLOCK == 0 and s % min(TOKEN_TILE, s) == 0
    assert (s // SSM_CHUNK) % min(SCAN_CHUNKS, s // SSM_CHUNK) == 0
    row = lambda v: v.reshape(1, -1)
    for i in range(depth):
        w = w_in[i].astype(BF16)
        o_q, o_k, o_v, o_g = wdt, wdt + a, wdt + 2 * a, wdt + 3 * a
        u5, qt, k, vt, ga, gb, kmean = _in_proj(
            x, row(g_pre_mix[i]), w[:, :o_q], w[:, o_q:o_k].T, w[:, o_k:o_v], w[:, o_v:o_g].T,
            w[:, o_g:], float(hd) ** -0.5 * LOG2E)

        prep = _ssm_prep(ssm_lam_re[i], ssm_lam_im[i], ssm_log_dt[i], ssm_b_re[i], ssm_b_im[i],
                         ssm_c_re[i], ssm_c_im[i])
        dvec = jnp.tile(ssm_d[i].reshape(-1, 1, LANES), (1, 1, SSM_CHUNK))
        yg5 = _s5(u5, *prep, dvec)

        at = _moba(qt, k, vt, kmean)

        x = _post(yg5, at, ga, gb, x, p[i],
                  [w_glu[i].astype(BF16), row(b_glu[i]), w_branch_a[i].astype(BF16),
                   w_branch_b[i].astype(BF16), w_out[i].astype(BF16), row(g_post_mix[i]),
                   row(g_pre_mlp[i]), w_mlp1[i].astype(BF16), w_mlp2[i].astype(BF16),
                   row(g_post_mlp[i]), w_ple[i].astype(BF16), w_ple_gate[i].astype(BF16),
                   row(g_ple[i])])
    return x
```

```python
import functools

import jax
import jax.numpy as jnp
from jax import lax
from jax.experimental import pallas as pl
from jax.experimental.pallas import tpu as pltpu

F32 = jnp.float32
BF16 = jnp.bfloat16

SSM_GROUP = 16
SSM_STATE = 64
ATTN_HEADS = 8
MOBA_BLOCK = 256
MOBA_TOPK = 3
RMS_EPS = 1e-6
NEG_INF = -1e30
LOG2E = 1.4426950408889634

LANES = 128
F32_SUBLANES = 8
VMEM_LIMIT = 48 * 1024 * 1024

SSM_CHUNK = F32_SUBLANES
TILE_GROUPS = LANES // SSM_GROUP
SCAN_CHUNKS = 32
S5_SUB_BLOCKS = 4
HEAD_PAIR = 2
MOBA_STEP_HEADS = 4
SUM_ROWS = 16
FLASH_CHAINS = 4
FLASH_LOOKAHEAD = 5
TOKEN_TILE = 1024
POST_TILE = 512
ROW_SUB = 256
FF_CHUNK = 1024

_NT = (((1,), (1,)), ((), ()))


def _rms(v):
    return v * lax.rsqrt(jnp.mean(v * v, axis=-1, keepdims=True) + RMS_EPS)


def _const_spec(shape):
    nd = len(shape)
    return pl.BlockSpec(shape, lambda *_: (0,) * nd, pipeline_mode=pl.Buffered(1))


def _params(sem):
    return pltpu.CompilerParams(dimension_semantics=sem, vmem_limit_bytes=VMEM_LIMIT)


def _chunk_tile_spec(n_tiles, tm):
    return pl.BlockSpec((n_tiles, tm // SSM_CHUNK, None, SSM_CHUNK, LANES),
                        lambda bi, ti: (0, ti, bi, 0, 0))


def _in_proj_kernel(x_ref, g_ref, wu_ref, wqt_ref, wk_ref, wvt_ref,
                    u_ref, qt_ref, k_ref, vt_ref, km_ref, *, q_scale):
    tm, d = x_ref.shape[1], x_ref.shape[2]
    sub = MOBA_BLOCK
    tiles = [slice(r, r + sub) for r in range(0, tm, sub)]
    norm = lambda rs: (_rms(x_ref[0, rs]) * g_ref[...]).astype(BF16)
    hbs = [norm(tiles[0])]
    for n, rs in enumerate(tiles):
        hb = hbs[n]
        if n + 1 < len(tiles):
            hbs.append(norm(tiles[n + 1]))
        cs = slice(rs.start // SSM_CHUNK, rs.stop // SSM_CHUNK)
        u = jnp.dot(hb, wu_ref[...], preferred_element_type=F32)
        for o in range(u_ref.shape[0]):
            u_ref[o, cs] = u[:, o * LANES:(o + 1) * LANES].reshape(sub // SSM_CHUNK, SSM_CHUNK, LANES)
        kf = jnp.dot(hb, wk_ref[...], preferred_element_type=F32)
        k_ref[0, rs] = kf.astype(BF16)
        km_ref[0, n] = jnp.mean(kf, axis=0, keepdims=True)
        qt = lax.dot_general(wqt_ref[...], hb, _NT, preferred_element_type=F32)
        qt_ref[0, :, rs] = (qt * q_scale).astype(BF16)
        vt_ref[0, :, rs] = lax.dot_general(wvt_ref[...], hb, _NT,
                                           preferred_element_type=F32).astype(BF16)


def _in_proj(x, g, wu, wqt, wk, wvt, q_scale):
    b, s, d = x.shape
    n_tiles = wu.shape[1] // LANES
    a = wk.shape[1]
    tm = min(TOKEN_TILE, s)
    tok = lambda w: pl.BlockSpec((1, tm, w), lambda bi, ti: (bi, ti, 0))
    tr = lambda w: pl.BlockSpec((1, w, tm), lambda bi, ti: (bi, 0, ti))
    return pl.pallas_call(
        functools.partial(_in_proj_kernel, q_scale=q_scale),
        grid=(b, s // tm),
        in_specs=[tok(d), _const_spec(g.shape), _const_spec(wu.shape), _const_spec(wqt.shape),
                  _const_spec(wk.shape), _const_spec(wvt.shape)],
        out_specs=[_chunk_tile_spec(n_tiles, tm), tr(a), tok(a), tr(a),
                   pl.BlockSpec((1, tm // MOBA_BLOCK, 1, a), lambda bi, ti: (bi, ti, 0, 0))],
        out_shape=[jax.ShapeDtypeStruct((n_tiles, s // SSM_CHUNK, b, SSM_CHUNK, LANES), F32),
                   jax.ShapeDtypeStruct((b, a, s), BF16), jax.ShapeDtypeStruct((b, s, a), BF16),
                   jax.ShapeDtypeStruct((b, a, s), BF16),
                   jax.ShapeDtypeStruct((b, s // MOBA_BLOCK, 1, a), F32)],
        compiler_params=_params(("parallel", "parallel")),
        name="in_proj",
    )(x, g, wu, wqt, wk, wvt)


def _ssm_prep_kernel(lre_ref, lim_ref, ldt_ref, btr_ref, bti_ref, cr_ref, ci_ref,
                     toep_ref, wst_ref, wout_ref, ltr_ref, lti_ref):
    t = SSM_CHUNK
    rows, width = cr_ref.shape[1], cr_ref.shape[2]
    lre = lre_ref[0]
    lim = lim_ref[0]
    dt = jnp.exp(ldt_ref[0])
    ea = lre * dt
    eb = lim * dt

    def power(tau):
        mag = jnp.exp(tau * ea)
        return mag * jnp.cos(tau * eb), mag * jnp.sin(tau * eb)

    def times(pw, mr, mi):
        return pw[0] * mr - pw[1] * mi, pw[0] * mi + pw[1] * mr

    pw = [power(float(tau)) for tau in range(t + 1)]
    ltr_ref[0], lti_ref[0] = pw[t]
    same_group = (lax.broadcasted_iota(jnp.int32, (rows, width), 0) // SSM_GROUP
                  == lax.broadcasted_iota(jnp.int32, (rows, width), 1) // SSM_STATE)
    keep = lambda ref: jnp.where(same_group, ref[0], 0.0)
    cr, ci = keep(cr_ref), keep(ci_ref)
    nr = pw[1][0] - 1.0
    ni = pw[1][1]
    den = lre * lre + lim * lim
    coef = ((nr * lre + ni * lim) / den, (ni * lre - nr * lim) / den)
    bbr, bbi = times(coef, keep(btr_ref), keep(bti_ref))

    for s in range(t):
        wr, wi = times(pw[t - 1 - s], bbr, bbi)
        wst_ref[0, s * rows:(s + 1) * rows, :] = jnp.concatenate([wr, wi], axis=1).astype(BF16)
        orr, oi = times(pw[s + 1], cr, ci)
        wout_ref[0, :width, s * rows:(s + 1) * rows] = orr.T.astype(BF16)
        wout_ref[0, width:, s * rows:(s + 1) * rows] = (-oi).T.astype(BF16)

    cl = [times(pw[tau], cr, ci) for tau in range(t)]
    clr = jnp.concatenate([m[0] for m in cl], axis=0)
    cli = jnp.concatenate([m[1] for m in cl], axis=0)
    hi = lax.Precision.HIGHEST
    bd = (lax.dot_general(bbr, clr, _NT, precision=hi, preferred_element_type=F32)
          - lax.dot_general(bbi, cli, _NT, precision=hi, preferred_element_type=F32)).astype(BF16)
    for s in range(t):
        lead = [jnp.zeros((rows, s * rows), BF16)] if s else []
        toep_ref[0, s * rows:(s + 1) * rows, :] = jnp.concatenate(
            lead + [bd[:, :(t - s) * rows]], axis=1)


def _ssm_prep(lam_re, lam_im, log_dt, b_re, b_im, c_re, c_im):
    g, p = lam_re.shape
    tg, t = TILE_GROUPS, SSM_CHUNK
    n = g // tg
    width = tg * p
    lanes = lambda v: v.reshape(n, 1, width)
    chan = lambda m: jnp.tile(m.reshape(n, LANES, p), (1, 1, tg))
    vec = pl.BlockSpec((1, 1, width), lambda ti: (ti, 0, 0))
    mat = pl.BlockSpec((1, LANES, width), lambda ti: (ti, 0, 0))
    sq = lambda r, c: pl.BlockSpec((1, r, c), lambda ti: (ti, 0, 0))
    return pl.pallas_call(
        _ssm_prep_kernel,
        grid=(n,),
        in_specs=[vec, vec, vec, mat, mat, mat, mat],
        out_specs=[sq(t * LANES, t * LANES), sq(t * LANES, 2 * width), sq(2 * width, t * LANES),
                   vec, vec],
        out_shape=[jax.ShapeDtypeStruct((n, t * LANES, t * LANES), BF16),
                   jax.ShapeDtypeStruct((n, t * LANES, 2 * width), BF16),
                   jax.ShapeDtypeStruct((n, 2 * width, t * LANES), BF16),
                   jax.ShapeDtypeStruct((n, 1, width), F32), jax.ShapeDtypeStruct((n, 1, width), F32)],
        compiler_params=_params(("parallel",)),
        name="ssm_prep",
    )(lanes(lam_re), lanes(lam_im), lanes(jnp.repeat(log_dt, p)),
      chan(jnp.swapaxes(b_re, 1, 2)), chan(jnp.swapaxes(b_im, 1, 2)), chan(c_re), chan(c_im))


def _s5_kernel(u_ref, m_ref, wst_ref, wout_ref, lr_ref, li_ref, d_ref, y_ref, h_ref, st_ref, *, nb):
    t = SSM_CHUNK
    rows = u_ref.shape[0] // t
    half = lr_ref.shape[-1]

    @pl.when(pl.program_id(1) == 0)
    def _():
        st_ref[...] = jnp.zeros(st_ref.shape, F32)

    rp = rows // S5_SUB_BLOCKS
    parts = [slice(p * rp, (p + 1) * rp) for p in range(S5_SUB_BLOCKS)]
    u32 = [jnp.concatenate([u_ref[pl.ds(rs.start * t + s, rp, stride=t), :] for s in range(t)], axis=1)
           for rs in parts]
    u = [v.astype(BF16) for v in u32]
    for rs, v in zip(parts, u):
        h_ref[rs, :] = jnp.dot(v, wst_ref[...], preferred_element_type=F32)
    lam_r = lr_ref[...]
    lam_i = li_ref[...]

    def step(k, carry):
        sr, si = carry
        rk = pl.ds(pl.multiple_of(k * nb, nb), nb)
        hr = h_ref[rk, :half]
        hi = h_ref[rk, half:]
        h_ref[rk, :half] = sr
        h_ref[rk, half:] = si
        return (lam_r * sr - lam_i * si + hr, lam_r * si + lam_i * sr + hi)

    y = [jnp.dot(v, m_ref[...], preferred_element_type=F32) + d_ref[...] * v32
         for v, v32 in zip(u, u32)]
    sr, si = lax.fori_loop(0, rows // nb, step, (st_ref[0], st_ref[1]), unroll=True)
    st_ref[0] = sr
    st_ref[1] = si
    y = [v + jnp.dot(h_ref[rs, :].astype(BF16), wout_ref[...], preferred_element_type=F32)
         for rs, v in zip(parts, y)]
    for rs, v in zip(parts, y):
        v = jax.nn.gelu(v)
        for s in range(t):
            y_ref[pl.ds(rs.start * t + s, rp, stride=t), :] = v[:, s * LANES:(s + 1) * LANES]


def _s5(u5, toep, wst, wout, lam_r, lam_i, dvec):
    n_tiles, n_chunks, nb, t, lanes = u5.shape
    kb = min(SCAN_CHUNKS, n_chunks)
    blk_tokens = kb * nb * t
    flat = u5.reshape(n_tiles, n_chunks * nb * t, lanes)
    width = toep.shape[-1]
    st = wst.shape[-1]
    tok = pl.BlockSpec((None, blk_tokens, lanes), lambda oi, ki: (oi, ki, 0))
    op = lambda r, c: pl.BlockSpec((None, r, c), lambda oi, ki: (oi, 0, 0))
    y = pl.pallas_call(
        functools.partial(_s5_kernel, nb=nb),
        grid=(n_tiles, n_chunks // kb),
        in_specs=[tok, op(width, width), op(width, st), op(st, width), op(1, st // 2), op(1, st // 2),
                  op(1, width)],
        out_specs=tok,
        out_shape=jax.ShapeDtypeStruct(flat.shape, F32),
        scratch_shapes=[pltpu.VMEM((kb * nb, st), F32), pltpu.VMEM((2, nb, st // 2), F32)],
        compiler_params=_params(("parallel", "arbitrary")),
        name="s5",
    )(flat, toep, wst, wout, lam_r, lam_i, dvec)
    return y.reshape(u5.shape)


def _moba_kernel(qt_ref, k_ref, vt_ref, km_ref, o_ref, *, n_blk, n_sel):
    blk = MOBA_BLOCK
    lanes = LANES
    hd = lanes // HEAD_PAIR
    n_heads = qt_ref.shape[1] // hd
    pair_lanes = lambda h: slice((h // HEAD_PAIR) * lanes, (h // HEAD_PAIR + 1) * lanes)
    gate_rows = 16
    means = [km_ref[0, j] for j in range(n_blk)]
    means.append(jnp.zeros((gate_rows - n_blk, k_ref.shape[2]), F32))
    kmean = jnp.concatenate(means, axis=0).astype(BF16)
    head_row = lax.broadcasted_iota(jnp.int32, (lanes, blk), 0) // hd
    blk_row = lax.broadcasted_iota(jnp.int32, (n_blk, blk), 0)
    causal = (lax.broadcasted_iota(jnp.int32, (blk, blk), 0)
              <= lax.broadcasted_iota(jnp.int32, (blk, blk), 1))
    ones = jnp.ones((SUM_ROWS, blk), BF16)
    mm = lambda a, w: jnp.dot(a, w, preferred_element_type=F32)

    state = {}

    def open_unit(i, h):
        qp = qt_ref[0, pair_lanes(h), i * blk:(i + 1) * blk]
        qh = jnp.where(head_row == h % HEAD_PAIR, qp, jnp.zeros_like(qp))
        bias = [None] * i
        if i > n_sel:
            gate = mm(kmean[:, pair_lanes(h)], qh)[:n_blk]
            gate = jnp.where(blk_row < i, gate, -jnp.inf)
            for n in range(i):
                gn = gate[n:n + 1, :]
                ahead = (gate > gn) | ((gate == gn) & (blk_row < n))
                rank = jnp.sum(ahead.astype(F32), axis=0, keepdims=True)
                bias[n] = jnp.where(rank < n_sel, 0.0, NEG_INF)
        state[(i, h)] = [qh, bias, None, None]

    def score(i, h, j):
        if (i, h) not in state:
            open_unit(i, h)
        return mm(k_ref[0, j * blk:(j + 1) * blk, pair_lanes(h)], state[(i, h)][0])

    def absorb(i, h, j, s):
        qh, bias, m_old, acc = state[(i, h)]
        bj = bias[j] if j < i else None
        if j == i:
            s = jnp.where(causal, s, NEG_INF)
        top = jnp.max(s, axis=0, keepdims=True)
        if bj is not None:
            top = top + bj
        m = top if m_old is None else jnp.maximum(m_old, top)
        p = jnp.exp2(s - (m if bj is None else m - bj)).astype(BF16)
        v = jnp.concatenate([vt_ref[0, h * hd:(h + 1) * hd, j * blk:(j + 1) * blk], ones], axis=0)
        o = mm(v, p)
        acc = o if acc is None else acc * jnp.exp2(m_old - m) + o
        state[(i, h)] = [qh, bias, m, acc]

    units = [(i, h) for i in reversed(range(n_blk)) for h in range(n_heads)]
    steps = []
    for w in range(0, len(units), FLASH_CHAINS):
        wave = units[w:w + FLASH_CHAINS]
        for t in range(max(i for i, _ in wave) + 1):
            steps += [(i, h, i - t) for i, h in wave if t <= i]
    pending = [score(*s) for s in steps[:FLASH_LOOKAHEAD]]
    for n, (i, h, j) in enumerate(steps):
        s = pending.pop(0)
        if n + FLASH_LOOKAHEAD < len(steps):
            pending.append(score(*steps[n + FLASH_LOOKAHEAD]))
        absorb(i, h, j, s)
        if j == 0 and h % HEAD_PAIR == HEAD_PAIR - 1:
            accs = [state.pop((i, hh))[3] for hh in range(h - HEAD_PAIR + 1, h + 1)]
            ot = jnp.concatenate([a[:hd] / a[hd:hd + 1] for a in accs], axis=0)
            o_ref[0, i * blk:(i + 1) * blk, pair_lanes(h)] = ot.T.astype(BF16)


def _moba(qt, k, vt, kmean):
    b, a, s = qt.shape
    lanes = MOBA_STEP_HEADS * (a // ATTN_HEADS)
    n_blk = s // MOBA_BLOCK
    tr = pl.BlockSpec((1, lanes, s), lambda bi, hi: (bi, hi, 0))
    tok = pl.BlockSpec((1, s, lanes), lambda bi, hi: (bi, 0, hi))
    return pl.pallas_call(
        functools.partial(_moba_kernel, n_blk=n_blk, n_sel=min(MOBA_TOPK, n_blk - 1)),
        grid=(b, a // lanes),
        in_specs=[tr, tok, tr, pl.BlockSpec((1, n_blk, 1, lanes), lambda bi, hi: (bi, 0, 0, hi))],
        out_specs=tok,
        out_shape=jax.ShapeDtypeStruct((b, s, a), BF16),
        compiler_params=_params(("parallel", "parallel")),
        name="moba",
    )(qt, k, vt, kmean)


def _post_kernel(yg_ref, at_ref, x_ref, p_ref, gin_ref, wg_ref, wglu_ref, bglu_ref, wa_ref, wb_ref,
                 wo_ref, gmix_ref, gpre_ref, w1_ref, w2_ref, gpost_ref, wple_ref, wpg_ref, gple_ref,
                 o_ref):
    tm, d = x_ref.shape[1], x_ref.shape[2]
    sub = min(ROW_SUB, tm)
    dff = w1_ref.shape[1]
    mm = lambda a, w: jnp.dot(a, w, preferred_element_type=F32)
    tiles = [slice(r, r + sub) for r in range(0, tm, sub)]
    chunks = [slice(rs.start // SSM_CHUNK, rs.stop // SSM_CHUNK) for rs in tiles]
    yg = [jnp.concatenate([yg_ref[o, cs].reshape(sub, LANES) for o in range(yg_ref.shape[0])], axis=1)
          for cs in chunks]
    glu = [mm(v.astype(BF16), wglu_ref[...]) + bglu_ref[...] for v in yg]
    hin = [(_rms(x_ref[0, rs]) * gin_ref[...]).astype(BF16) for rs in tiles]
    gates = [jax.nn.sigmoid(mm(h, wg_ref[...])) for h in hin]
    yb = [mm(at_ref[0, rs], wb_ref[...]) for rs in tiles]
    ya_in = [(v * jax.nn.sigmoid(s)).astype(BF16) for v, s in zip(yg, glu)]
    ya = [mm(v, wa_ref[...]) for v in ya_in]
    mixed = [(g[:, :d] * a + g[:, d:] * b).astype(BF16) for g, a, b in zip(gates, ya, yb)]
    mo = [mm(v, wo_ref[...]) for v in mixed]
    e = [mm(p_ref[0, rs].astype(BF16), wple_ref[...]) for rs in tiles]
    xs = [x_ref[0, rs] + _rms(v) * gmix_ref[...] for rs, v in zip(tiles, mo)]

    hm = [(_rms(x) * gpre_ref[...]).astype(BF16) for x in xs]
    f = [None] * len(tiles)
    for c in range(0, dff, FF_CHUNK):
        hid = [mm(h, w1_ref[:, c:c + FF_CHUNK]) for h in hm]
        hid = [jnp.square(jnp.maximum(h, 0.0)).astype(BF16) for h in hid]
        part = [mm(h, w2_ref[c:c + FF_CHUNK, :]) for h in hid]
        f = [v if acc is None else acc + v for acc, v in zip(f, part)]
    xs = [x + _rms(v) * gpost_ref[...] for x, v in zip(xs, f)]

    gate = [mm(x.astype(BF16), wpg_ref[...]) for x in xs]
    for rs, x, ev, gv in zip(tiles, xs, e, gate):
        o_ref[0, rs] = x + _rms(ev * jax.nn.sigmoid(gv)) * gple_ref[...]


def _post(yg5, at, x, p, consts):
    b, s, d = x.shape
    tm = min(POST_TILE, s)
    tok = lambda w: pl.BlockSpec((1, tm, w), lambda bi, ti: (bi, ti, 0))
    return pl.pallas_call(
        _post_kernel,
        grid=(b, s // tm),
        in_specs=[_chunk_tile_spec(yg5.shape[0], tm), tok(at.shape[2]), tok(d), tok(p.shape[2])]
        + [_const_spec(c.shape) for c in consts],
        out_specs=tok(d),
        out_shape=jax.ShapeDtypeStruct((b, s, d), F32),
        compiler_params=_params(("parallel", "parallel")),
        name="post",
    )(yg5, at, x, p, *consts)


def kernel(x, p, g_pre_mix, w_in, ssm_lam_re, ssm_lam_im, ssm_log_dt, ssm_b_re, ssm_b_im, ssm_c_re,
           ssm_c_im, ssm_d, w_glu, b_glu, w_branch_a, w_branch_b, w_out, g_post_mix, g_pre_mlp,
           w_mlp1, w_mlp2, g_post_mlp, w_ple, w_ple_gate, g_ple):
    b, s, d = x.shape
    depth = w_in.shape[0]
    wdt = w_glu.shape[1]
    a = w_branch_b.shape[1]
    hd = a // ATTN_HEADS
    assert hd * HEAD_PAIR == LANES and ATTN_HEADS % MOBA_STEP_HEADS == 0
    assert wdt % LANES == 0 and s % MOBA_BLOCK == 0 and s % min(TOKEN_TILE, s) == 0
    assert (s // SSM_CHUNK) % min(SCAN_CHUNKS, s // SSM_CHUNK) == 0
    row = lambda v: v.reshape(1, -1)
    for i in range(depth):
        w = w_in[i].astype(BF16)
        o_q, o_k, o_v, o_g = wdt, wdt + a, wdt + 2 * a, wdt + 3 * a
        u5, qt, k, vt, kmean = _in_proj(
            x, row(g_pre_mix[i]), w[:, :o_q], w[:, o_q:o_k].T, w[:, o_k:o_v], w[:, o_v:o_g].T,
            float(hd) ** -0.5 * LOG2E)

        prep = _ssm_prep(ssm_lam_re[i], ssm_lam_im[i], ssm_log_dt[i], ssm_b_re[i], ssm_b_im[i],
                         ssm_c_re[i], ssm_c_im[i])
        dvec = jnp.tile(ssm_d[i].reshape(-1, 1, LANES), (1, 1, SSM_CHUNK))
        yg5 = _s5(u5, *prep, dvec)

        at = _moba(qt, k, vt, kmean)

        x = _post(yg5, at, x, p[i],
                  [row(g_pre_mix[i]), w[:, o_g:],
                   w_glu[i].astype(BF16), row(b_glu[i]), w_branch_a[i].astype(BF16),
                   w_branch_b[i].astype(BF16), w_out[i].astype(BF16), row(g_post_mix[i]),
                   row(g_pre_mlp[i]), w_mlp1[i].astype(BF16), w_mlp2[i].astype(BF16),
                   row(g_post_mlp[i]), w_ple[i].astype(BF16), w_ple_gate[i].astype(BF16),
                   row(g_ple[i])])
    return x
```
